```python
import jax
import jax.numpy as jnp
from jax import lax
import numpy as np

D_MODEL = 2048
BATCH = 2
SEQ = 16384
DEPTH = 4

GRID_W = 64
CTX_LEN = 256
CHUNK = 128

M_HEADS = 4
M_HEAD_DIM = 256
M_WIDTH = M_HEADS * M_HEAD_DIM
M_CONV = 5
M_GATE_COLS = 2 * 2 * M_HEADS
R_HEADS = 4
R_HEAD_DIM = 256
R_WIDTH = R_HEADS * R_HEAD_DIM
ROPE_BASE = 10000.0
S_GROUPS = 8
S_GROUP_DIM = 128
S_WIDTH = S_GROUPS * S_GROUP_DIM
D_FF = -(-8 * D_MODEL // (3 * 256)) * 256

IN_SIZES = (M_WIDTH,) * 4 + (M_GATE_COLS,) + (R_WIDTH,) * 4 + (S_WIDTH,) * 2 + (D_MODEL,) * 3
IN_COLS = sum(IN_SIZES)
_SPLITS = tuple(int(o) for o in np.cumsum(IN_SIZES)[:-1])
NORM_EPS = 1e-6

kernel_name = 'hybrid_mlstm_retention_chunkmlp_prefix_dit'


def rmsnorm(x, g):
    xf = x.astype(jnp.float32)
    y = xf * lax.rsqrt(jnp.mean(xf * xf, axis=-1, keepdims=True) + NORM_EPS)
    return (y * g.astype(jnp.float32)).astype(x.dtype)


def headnorm(y, g):
    yc = y - jnp.mean(y, axis=-1, keepdims=True)
    yn = yc * lax.rsqrt(jnp.mean(yc * yc, axis=-1, keepdims=True) + NORM_EPS)
    return yn.reshape(y.shape[:2] + (-1,)) * g.astype(jnp.float32)


def modulate(h, shift, scale):
    return h * (1 + scale) + shift


def swiglu(h, w_gate, w_up, w_down):
    return (jax.nn.silu(h @ w_gate) * (h @ w_up)) @ w_down


def dwconv(x, w):
    k = w.shape[0]
    return lax.conv_general_dilated(x, w[:, None, :].astype(x.dtype), window_strides=(1,), padding=[(k // 2, k // 2)], dimension_numbers=('NWC', 'WIO', 'NWC'), feature_group_count=x.shape[-1])


def rope_2d(n_tokens):
    rows = n_tokens // GRID_W
    half = R_HEAD_DIM // 4
    freq = ROPE_BASE ** (-jnp.arange(half, dtype=jnp.float32) / half)
    ang_r = jnp.arange(rows, dtype=jnp.float32)[:, None] * freq
    ang_c = jnp.arange(GRID_W, dtype=jnp.float32)[:, None] * freq
    ang = jnp.concatenate([jnp.broadcast_to(ang_r[:, None, :], (rows, GRID_W, half)), jnp.broadcast_to(ang_c[None, :, :], (rows, GRID_W, half))], axis=-1).reshape(rows * GRID_W, 2 * half)
    return jnp.cos(ang), jnp.sin(ang)


def apply_rope(x, cos, sin):
    x1, x2 = jnp.split(x, 2, axis=-1)
    cos = cos[None, :, None, :]
    sin = sin[None, :, None, :]
    return jnp.concatenate([x1 * cos - x2 * sin, x2 * cos + x1 * sin], axis=-1)


def _chunks(a):
    b, t, h = a.shape[:3]
    a = a.reshape((b, t // CHUNK, CHUNK, h) + a.shape[3:])
    return jnp.moveaxis(a, 3, 1)


def _unchunks(a):
    b, h, nc, l = a.shape[:4]
    return jnp.moveaxis(a, 1, 3).reshape((b, nc * l, h) + a.shape[4:])


def _rev(a, d):
    return a[:, ::-1] if d == 1 else a


def project(hn, w_in, conv_w, gate_b):
    b, t, _ = hn.shape
    z = hn @ w_in
    mq, mk, mv, mo, mg, rq, rk, rv, rg, su, sv, gm, gr, gs = jnp.split(z, _SPLITS, axis=-1)
    mq, mk = jnp.split(jax.nn.silu(dwconv(jnp.concatenate([mq, mk], axis=-1), conv_w)), 2, axis=-1)

    def heads(a, n, dh):
        return a.astype(jnp.float32).reshape(b, t, n, dh)

    return {
        'm_q': heads(mq, M_HEADS, M_HEAD_DIM),
        'm_k': heads(mk, M_HEADS, M_HEAD_DIM) * (M_HEAD_DIM ** -0.5),
        'm_v': heads(mv, M_HEADS, M_HEAD_DIM),
        'm_o': jax.nn.sigmoid(heads(mo, M_HEADS, M_HEAD_DIM)),
        'm_gates': mg.astype(jnp.float32).reshape(b, t, 2, 2, M_HEADS) + gate_b.astype(jnp.float32),
        'r_q': heads(rq, R_HEADS, R_HEAD_DIM) * (R_HEAD_DIM ** -0.5),
        'r_k': heads(rk, R_HEADS, R_HEAD_DIM),
        'r_v': heads(rv, R_HEADS, R_HEAD_DIM),
        'r_g': rg,
        's_u': su,
        's_v': sv,
        'gates': (gm, gr, gs),
    }


def mlstm_states(k, v, log_i, log_f, state):
    bcum = jnp.cumsum(log_f, axis=-1)
    b_end = bcum[..., -1]
    a = b_end[..., None] - bcum + log_i

    def step(carry, inp):
        c, n, m = carry
        kc, vc, ac, bc = inp
        m_new = jnp.maximum(bc + m, jnp.max(ac, axis=-1))
        decay = jnp.exp(bc + m - m_new)
        w = jnp.exp(ac - m_new[..., None])
        c_new = decay[..., None, None] * c + jnp.einsum('bhl,bhld,bhle->bhde', w, kc, vc)
        n_new = decay[..., None] * n + jnp.einsum('bhl,bhld->bhd', w, kc)
        return (c_new, n_new, m_new), (c, n, m)

    xs = (jnp.moveaxis(k, 2, 0), jnp.moveaxis(v, 2, 0), jnp.moveaxis(a, 2, 0), jnp.moveaxis(b_end, 2, 0))
    final, before = lax.scan(step, state, xs)
    before = jax.tree_util.tree_map(lambda s: jnp.moveaxis(s, 0, 2), before)
    return before, final


def mlstm_readout(q, k, v, log_i, log_f, before):
    c, n, m = before
    l = q.shape[3]
    bcum = jnp.cumsum(log_f, axis=-1)
    causal = jnp.tril(jnp.ones((l, l), dtype=bool))
    dmat = jnp.where(causal, bcum[..., :, None] - bcum[..., None, :] + log_i[..., None, :], -jnp.inf)
    inter = bcum + m[..., None]
    m_t = jnp.maximum(inter, jnp.max(dmat, axis=-1))
    s = jnp.einsum('bhcld,bhcsd->bhcls', q, k) * jnp.exp(dmat - m_t[..., None])
    w_inter = jnp.exp(inter - m_t)
    num = jnp.einsum('bhcls,bhcse->bhcle', s, v) + w_inter[..., None] * jnp.einsum('bhcld,bhcde->bhcle', q, c)
    den = jnp.sum(s, axis=-1) + w_inter * jnp.einsum('bhcld,bhcd->bhcl', q, n)
    return num / jnp.maximum(jnp.abs(den), jnp.exp(-m_t))[..., None]


def retention_states(k, v, log_g, state):
    l = k.shape[3]
    pos = jnp.arange(l, dtype=jnp.float32)
    w_end = jnp.exp(log_g[:, None] * (l - 1 - pos))
    decay = jnp.exp(log_g * l)[None, :, None, None]

    def step(r, inp):
        kc, vc = inp
        return decay * r + jnp.einsum('hl,bhld,bhle->bhde', w_end, kc, vc), r

    final, before = lax.scan(step, state, (jnp.moveaxis(k, 2, 0), jnp.moveaxis(v, 2, 0)))
    return jnp.moveaxis(before, 0, 2), final


def retention_readout(q, k, v, log_g, before):
    l = q.shape[3]
    pos = jnp.arange(l, dtype=jnp.float32)
    rel = pos[:, None] - pos[None, :]
    dmask = jnp.where(rel >= 0, jnp.exp(log_g[:, None, None] * jnp.maximum(rel, 0.0)), 0.0)
    s = jnp.einsum('bhcld,bhcsd->bhcls', q, k) * dmask[:, None]
    inner = jnp.einsum('bhcls,bhcse->bhcle', s, v)
    cross = jnp.einsum('bhcld,bhcde->bhcle', q, before) * jnp.exp(log_g[:, None] * (pos + 1.0))[:, None, :, None]
    return inner + cross


def _mlstm_inputs(s, d):
    g = s['m_gates']
    seq = (s['m_q'], s['m_k'], s['m_v'], g[:, :, d, 0], jax.nn.log_sigmoid(g[:, :, d, 1]))
    return tuple(_chunks(_rev(a, d)) for a in seq)


def mlstm_branch(lat, cx, norm_g, with_ctx_out):
    b = lat['m_q'].shape[0]
    state0 = (jnp.zeros((b, M_HEADS, M_HEAD_DIM, M_HEAD_DIM), jnp.float32), jnp.zeros((b, M_HEADS, M_HEAD_DIM), jnp.float32), jnp.full((b, M_HEADS), -jnp.inf, jnp.float32))
    h_lat, h_ctx = [], []
    for d in range(2):
        cq, ck, cv, ci, cf = _mlstm_inputs(cx, d)
        lq, lk, lv, li, lf = _mlstm_inputs(lat, d)
        ctx_before, ctx_final = mlstm_states(ck, cv, ci, cf, state0)
        lat_before, _ = mlstm_states(lk, lv, li, lf, ctx_final)
        h_lat.append(_rev(_unchunks(mlstm_readout(lq, lk, lv, li, lf, lat_before)), d))
        if with_ctx_out:
            h_ctx.append(_rev(_unchunks(mlstm_readout(cq, ck, cv, ci, cf, ctx_before)), d))
    y_lat = headnorm(lat['m_o'] * (h_lat[0] + h_lat[1]), norm_g)
    y_ctx = headnorm(cx['m_o'] * (h_ctx[0] + h_ctx[1]), norm_g) if with_ctx_out else None
    return y_lat, y_ctx


def retention_branch(lat, cx, decay_logit, norm_g, with_ctx_out):
    b = lat['r_q'].shape[0]
    state0 = jnp.zeros((b, R_HEADS, R_HEAD_DIM, R_HEAD_DIM), jnp.float32)
    names = ('r_q', 'r_k', 'r_v')
    h_lat, h_ctx = [], []
    for d in range(2):
        log_g = jax.nn.log_sigmoid(decay_logit[d].astype(jnp.float32))
        cq, ck, cv = tuple(_chunks(_rev(cx[nm], d)) for nm in names)
        lq, lk, lv = tuple(_chunks(_rev(lat[nm], d)) for nm in names)
        ctx_before, ctx_final = retention_states(ck, cv, log_g, state0)
        lat_before, _ = retention_states(lk, lv, log_g, ctx_final)
        h_lat.append(_rev(_unchunks(retention_readout(lq, lk, lv, log_g, lat_before)), d))
        if with_ctx_out:
            h_ctx.append(_rev(_unchunks(retention_readout(cq, ck, cv, log_g, ctx_before)), d))
    y_lat = headnorm(h_lat[0] + h_lat[1], norm_g) * jax.nn.silu(lat['r_g'].astype(jnp.float32))
    y_ctx = headnorm(h_ctx[0] + h_ctx[1], norm_g) * jax.nn.silu(cx['r_g'].astype(jnp.float32)) if with_ctx_out else None
    return y_lat, y_ctx


def chunk_mlp(u, v, s_w, s_b):
    b, t, _ = v.shape
    shape = (b, t // CHUNK, CHUNK, S_GROUPS, S_GROUP_DIM)
    vf = v.astype(jnp.float32).reshape(shape)
    vc = vf - jnp.mean(vf, axis=-1, keepdims=True)
    vn = vc * lax.rsqrt(jnp.mean(vc * vc, axis=-1, keepdims=True) + NORM_EPS)
    mixed = jnp.einsum('gpq,bnqgc->bnpgc', s_w.astype(jnp.float32), vn) + s_b.astype(jnp.float32).T[None, None, :, :, None]
    return (u.astype(jnp.float32).reshape(shape) * mixed).reshape(b, t, S_WIDTH)


def merge(ys, gates, w_up_m, w_up_r, w_up_s, w_out, dtype):
    y_m, y_r, y_s = ys
    g_m, g_r, g_s = gates
    y = (jax.nn.sigmoid(g_m) * (y_m.astype(dtype) @ w_up_m)
         + jax.nn.sigmoid(g_r) * (y_r.astype(dtype) @ w_up_r)
         + jax.nn.sigmoid(g_s) * (y_s.astype(dtype) @ w_up_s))
    return y @ w_out


def mixer(h, hc, cos, sin, w_in, m_gate_b, m_conv_w, m_norm_g, r_decay_logit, r_norm_g, s_w, s_b, w_up_m, w_up_r, w_up_s, w_out, with_ctx_out):
    lat = project(h, w_in, m_conv_w, m_gate_b)
    cx = project(hc, w_in, m_conv_w, m_gate_b)
    lat['r_q'] = apply_rope(lat['r_q'], cos, sin)
    lat['r_k'] = apply_rope(lat['r_k'], cos, sin)
    y_m, y_mc = mlstm_branch(lat, cx, m_norm_g, with_ctx_out)
    y_r, y_rc = retention_branch(lat, cx, r_decay_logit, r_norm_g, with_ctx_out)
    y_s = chunk_mlp(lat['s_u'], lat['s_v'], s_w, s_b)
    y = merge((y_m, y_r, y_s), lat['gates'], w_up_m, w_up_r, w_up_s, w_out, h.dtype)
    if not with_ctx_out:
        return y, None
    y_sc = chunk_mlp(cx['s_u'], cx['s_v'], s_w, s_b)
    yc = merge((y_mc, y_rc, y_sc), cx['gates'], w_up_m, w_up_r, w_up_s, w_out, hc.dtype)
    return y, yc


def setup_inputs(seed: int = 0) -> dict:
    key = jax.random.key(seed)
    ks = jax.random.split(key, 24)
    f32 = jnp.float32

    def nrm(k, shape, scale):
        return jax.random.normal(k, shape, f32) * scale

    def gain(k, shape):
        return 1.0 + nrm(k, shape, 0.02)

    gate_bias_base = jnp.stack([jnp.zeros((M_HEADS,), f32), jnp.linspace(3.0, 6.0, M_HEADS, dtype=f32)])
    decay_base = jnp.log(2.0 ** (5.0 + jnp.arange(R_HEADS, dtype=f32)) - 1.0)
    return {
        'x': nrm(ks[0], (BATCH, SEQ, D_MODEL), 1.0),
        'c': nrm(ks[1], (BATCH, D_MODEL), 1.0),
        'ctx': nrm(ks[2], (BATCH, CTX_LEN, D_MODEL), 1.0),
        'c_ctx': nrm(ks[3], (D_MODEL,), 1.0),
        'ada_w': nrm(ks[4], (DEPTH, D_MODEL, 6 * D_MODEL), 0.5 * D_MODEL ** -0.5),
        'ada_b': nrm(ks[5], (DEPTH, 6 * D_MODEL), 0.01),
        'norm_mix_g': gain(ks[6], (DEPTH, D_MODEL)),
        'norm_ffn_g': gain(ks[7], (DEPTH, D_MODEL)),
        'w_in': nrm(ks[8], (DEPTH, D_MODEL, IN_COLS), D_MODEL ** -0.5),
        'm_gate_b': nrm(ks[9], (DEPTH, 2, 2, M_HEADS), 0.1) + gate_bias_base,
        'm_conv_w': nrm(ks[10], (DEPTH, M_CONV, 2 * M_WIDTH), M_CONV ** -0.5),
        'm_norm_g': gain(ks[11], (DEPTH, M_WIDTH)),
        'r_decay_logit': decay_base + nrm(ks[12], (DEPTH, 2, R_HEADS), 0.02),
        'r_norm_g': gain(ks[13], (DEPTH, R_WIDTH)),
        's_w': nrm(ks[14], (DEPTH, S_GROUPS, CHUNK, CHUNK), CHUNK ** -0.5),
        's_b': gain(ks[15], (DEPTH, S_GROUPS, CHUNK)),
        'w_up_m': nrm(ks[16], (DEPTH, M_WIDTH, D_MODEL), M_WIDTH ** -0.5),
        'w_up_r': nrm(ks[17], (DEPTH, R_WIDTH, D_MODEL), R_WIDTH ** -0.5),
        'w_up_s': nrm(ks[18], (DEPTH, S_WIDTH, D_MODEL), S_WIDTH ** -0.5),
        'w_out': nrm(ks[19], (DEPTH, D_MODEL, D_MODEL), D_MODEL ** -0.5),
        'ffn_w_gate': nrm(ks[20], (DEPTH, D_MODEL, D_FF), D_MODEL ** -0.5),
        'ffn_w_up': nrm(ks[21], (DEPTH, D_MODEL, D_FF), D_MODEL ** -0.5),
        'ffn_w_down': nrm(ks[22], (DEPTH, D_FF, D_MODEL), D_FF ** -0.5),
        'final_norm_g': gain(ks[23], (D_MODEL,)),
    }


def reference(x, c, ctx, c_ctx, ada_w, ada_b, norm_mix_g, norm_ffn_g, w_in, m_gate_b, m_conv_w, m_norm_g, r_decay_logit, r_norm_g, s_w, s_b, w_up_m, w_up_r, w_up_s, w_out, ffn_w_gate, ffn_w_up, ffn_w_down, final_norm_g):
    cos, sin = rope_2d(x.shape[1])
    sc = jax.nn.silu(c)
    scc = jax.nn.silu(c_ctx)
    for l in range(DEPTH):
        last = l == DEPTH - 1
        mod = jnp.split((sc @ ada_w[l] + ada_b[l])[:, None, :], 6, axis=-1)
        mod_c = jnp.split((scc @ ada_w[l] + ada_b[l])[None, None, :], 6, axis=-1)
        h = modulate(rmsnorm(x, norm_mix_g[l]), mod[0], mod[1])
        hc = modulate(rmsnorm(ctx, norm_mix_g[l]), mod_c[0], mod_c[1])
        y, yc = mixer(h, hc, cos, sin, w_in[l], m_gate_b[l], m_conv_w[l], m_norm_g[l], r_decay_logit[l], r_norm_g[l], s_w[l], s_b[l], w_up_m[l], w_up_r[l], w_up_s[l], w_out[l], not last)
        x = x + mod[2] * y
        h = modulate(rmsnorm(x, norm_ffn_g[l]), mod[3], mod[4])
        x = x + mod[5] * swiglu(h, ffn_w_gate[l], ffn_w_up[l], ffn_w_down[l])
        if not last:
            ctx = ctx + mod_c[2] * yc
            hc = modulate(rmsnorm(ctx, norm_ffn_g[l]), mod_c[3], mod_c[4])
            ctx = ctx + mod_c[5] * swiglu(hc, ffn_w_gate[l], ffn_w_up[l], ffn_w_down[l])
    return rmsnorm(x, final_norm_g)
```

```python
import functools

import jax
import jax.numpy as jnp
from jax import lax
from jax.experimental import pallas as pl
from jax.experimental.pallas import tpu as pltpu

F32 = jnp.float32
BF16 = jnp.bfloat16

CHUNK = 128
GRID_W = 64
HEADS = 4
HEAD_DIM = 256
WIDTH = HEADS * HEAD_DIM
CONV_TAPS = 5
GATE_COLS = 16
GATE_PAD = 128
S_GROUPS = 8
S_GROUP_DIM = 128
ROPE_BASE = 10000.0
NORM_EPS = 1e-6
HALO = 8
MOD_ROWS = 8

ROW_TILE = 512
VMEM_LIMIT = 56 * 1024 * 1024

COL_MQ, COL_MK, COL_MV, COL_MO, COL_RQ, COL_RK, COL_RV, COL_RG, COL_SU, COL_SV = range(10)
COL_GATES = 10


def _sigmoid(x):
    return 1.0 / (1.0 + jnp.exp(-x))


def _silu(x):
    return x / (1.0 + jnp.exp(-x))


def _log_sigmoid(x):
    return jnp.minimum(x, 0.0) - jnp.log(1.0 + jnp.exp(-jnp.abs(x)))


def _dot(a, b):
    return jnp.dot(a, b, preferred_element_type=F32)


def _dot_nt(a, b):
    return lax.dot_general(a, b, (((1,), (1,)), ((), ())), preferred_element_type=F32)


def _dot_tn(a, b):
    return lax.dot_general(a, b, (((0,), (0,)), ((), ())), preferred_element_type=F32)


def _params(sem, vmem=VMEM_LIMIT):
    return pltpu.CompilerParams(dimension_semantics=sem, vmem_limit_bytes=vmem)


def _ada_kernel(s_ref, w_ref, b_ref, o_ref, *, owners):
    s = _silu(s_ref[...])
    w = w_ref[0]
    rows = [jnp.sum(w * s[:, m:m + 1], axis=0, keepdims=True) for m in range(owners)]
    rows.append(jnp.zeros((MOD_ROWS - owners, w.shape[1]), F32))
    o_ref[0] = jnp.concatenate(rows, axis=0) + b_ref[0]


def _ada_call(cond_t, ada_w, ada_b, owners):
    depth, d, n = ada_w.shape
    tn = 512
    return pl.pallas_call(
        functools.partial(_ada_kernel, owners=owners),
        grid=(depth, n // tn),
        in_specs=[
            pl.BlockSpec((d, MOD_ROWS), lambda l, j: (0, 0)),
            pl.BlockSpec((1, d, tn), lambda l, j: (l, 0, j)),
            pl.BlockSpec((1, 1, tn), lambda l, j: (l, 0, j)),
        ],
        out_specs=pl.BlockSpec((1, MOD_ROWS, tn), lambda l, j: (l, 0, j)),
        out_shape=jax.ShapeDtypeStruct((depth, MOD_ROWS, n), F32),
        compiler_params=_params(("arbitrary", "arbitrary")),
        name="adaln",
    )(cond_t, ada_w, ada_b.reshape(depth, 1, n))


def _norm_mod(x, g, shift, scale):
    y = x * lax.rsqrt(jnp.mean(x * x, axis=-1, keepdims=True) + NORM_EPS) * g
    return y * (1.0 + scale) + shift


def _inproj_kernel(x_ref, mod_ref, g_ref, w_ref, wg_ref, z_ref, gate_ref, hn_ref):
    @pl.when(pl.program_id(1) == 0)
    def _():
        h = _norm_mod(x_ref[...], g_ref[...], mod_ref[0, 0:1, :], mod_ref[0, 1:2, :])
        hb = h.astype(BF16)
        hn_ref[...] = hb
        gate_ref[...] = _dot(hb, wg_ref[...])

    z_ref[...] = _dot(hn_ref[...], w_ref[...])


def _inproj_call(xf, mod_l, g, w_main, w_gate, owner_of_tile):
    rows, d = xf.shape
    n = w_main.shape[1]
    tm, tn = ROW_TILE, 1024
    return pl.pallas_call(
        _inproj_kernel,
        grid=(rows // tm, n // tn),
        in_specs=[
            pl.BlockSpec((tm, d), lambda m, j: (m, 0)),
            pl.BlockSpec((1, MOD_ROWS, d), lambda m, j: (owner_of_tile(m), 0, 0)),
            pl.BlockSpec((1, d), lambda m, j: (0, 0)),
            pl.BlockSpec((d, tn), lambda m, j: (0, j)),
            pl.BlockSpec((d, GATE_PAD), lambda m, j: (0, 0)),
        ],
        out_specs=[
            pl.BlockSpec((tm, tn), lambda m, j: (m, j)),
            pl.BlockSpec((tm, GATE_PAD), lambda m, j: (m, 0)),
        ],
        out_shape=[
            jax.ShapeDtypeStruct((rows, n), F32),
            jax.ShapeDtypeStruct((rows, GATE_PAD), F32),
        ],
        scratch_shapes=[pltpu.VMEM((tm, d), BF16)],
        compiler_params=_params(("arbitrary", "arbitrary")),
        name="inproj",
    )(xf, mod_l, g, w_main, w_gate)


def _gateprep_kernel(g_ref, b_ref, col_ref, row_ref):
    g = g_ref[...] + b_ref[...]
    lane = lax.broadcasted_iota(jnp.int32, (CHUNK, GATE_PAD), 1)
    t = lax.broadcasted_iota(jnp.int32, (CHUNK, GATE_PAD), 0)
    is_forget = (lane % 8) >= HEADS
    is_backward = lane >= 8
    lf = _log_sigmoid(g)
    pre = lf
    suf = lf
    s = 1
    while s < CHUNK:
        pre = pre + jnp.where(t >= s, pltpu.roll(pre, s, 0), 0.0)
        suf = suf + jnp.where(t < CHUNK - s, pltpu.roll(suf, CHUNK - s, 0), 0.0)
        s *= 2
    col = jnp.where(is_forget, jnp.where(is_backward, suf, pre), g)
    col_ref[...] = col
    row_ref[0] = col.T[0:GATE_COLS, :]


def _gateprep_call(gates, bias_row):
    rows = gates.shape[0]
    nchunks = rows // CHUNK
    return pl.pallas_call(
        _gateprep_kernel,
        grid=(nchunks,),
        in_specs=[
            pl.BlockSpec((CHUNK, GATE_PAD), lambda i: (i, 0)),
            pl.BlockSpec((1, GATE_PAD), lambda i: (0, 0)),
        ],
        out_specs=[
            pl.BlockSpec((CHUNK, GATE_PAD), lambda i: (i, 0)),
            pl.BlockSpec((1, GATE_COLS, CHUNK), lambda i: (i, 0, 0)),
        ],
        out_shape=[
            jax.ShapeDtypeStruct((rows, GATE_PAD), F32),
            jax.ShapeDtypeStruct((nchunks, GATE_COLS, CHUNK), F32),
        ],
        compiler_params=_params(("arbitrary",)),
        name="gateprep",
    )(gates, bias_row)


class _ScanGeom:
    def __init__(self, batch, t, ctx_len):
        self.batch = batch
        self.nlat = t // CHUNK
        self.nctx = ctx_len // CHUNK
        self.steps = self.nlat + self.nctx
        self.total_chunks = batch * self.steps

    def chunk_block(self, d, b, j):
        if d == 0:
            ctx_i, lat_i = j, j - self.nctx
        else:
            ctx_i, lat_i = self.nctx - 1 - j, self.steps - 1 - j
        return jnp.where(j < self.nctx,
                         self.batch * self.nlat + b * self.nctx + ctx_i,
                         b * self.nlat + lat_i)

    def lat_chunk(self, d, j):
        lat_i = j - self.nctx if d == 0 else self.steps - 1 - j
        return jnp.clip(lat_i, 0, self.nlat - 1)

    def seq_edges(self, d, j):
        starts = (j == 0) | (j == self.nctx)
        ends = (j == self.nctx - 1) | (j == self.steps - 1)
        return (starts, ends) if d == 0 else (ends, starts)


def _headnorm(y, g):
    yc = y - jnp.mean(y, axis=-1, keepdims=True)
    return yc * lax.rsqrt(jnp.mean(yc * yc, axis=-1, keepdims=True) + NORM_EPS) * g


def _causal_mask(d):
    t = lax.broadcasted_iota(jnp.int32, (CHUNK, CHUNK), 0)
    s = lax.broadcasted_iota(jnp.int32, (CHUNK, CHUNK), 1)
    return (t >= s) if d == 0 else (s >= t)


def _mlstm_kernel(*refs, d, final, geom):
    if final:
        (qp_ref, q_ref, qn_ref, kp_ref, k_ref, kn_ref, v_ref, col_ref, row_ref, cw_ref,
         hprev_ref, mo_ref, ng_ref, out_ref, c_ref, n_ref, m_ref, qext_ref, kext_ref) = refs
    else:
        (qp_ref, q_ref, qn_ref, kp_ref, k_ref, kn_ref, v_ref, col_ref, row_ref, cw_ref,
         out_ref, c_ref, n_ref, m_ref, qext_ref, kext_ref) = refs
    j = pl.program_id(1)

    @pl.when(j == 0)
    def _():
        c_ref[...] = jnp.zeros_like(c_ref)
        n_ref[...] = jnp.zeros_like(n_ref)
        m_ref[...] = jnp.full(m_ref.shape, -jnp.inf, F32)

    first, last = geom.seq_edges(d, j)
    keep_prev = jnp.where(first, 0.0, 1.0)
    keep_next = jnp.where(last, 0.0, 1.0)

    def conv_silu(p_ref, cur_ref, nx_ref, ext_ref, w_off):
        ext_ref[0:HALO, :] = p_ref[...] * keep_prev
        ext_ref[HALO:HALO + CHUNK, :] = cur_ref[...]
        ext_ref[HALO + CHUNK:HALO + CHUNK + HALO, :] = nx_ref[...] * keep_next
        acc = None
        for tap in range(CONV_TAPS):
            start = HALO - CONV_TAPS // 2 + tap
            term = ext_ref[start:start + CHUNK, :] * cw_ref[tap:tap + 1, w_off:w_off + WIDTH]
            acc = term if acc is None else acc + term
        return _silu(acc)

    q = conv_silu(qp_ref, q_ref, qn_ref, qext_ref, 0)
    k = conv_silu(kp_ref, k_ref, kn_ref, kext_ref, WIDTH) * (HEAD_DIM ** -0.5)
    col = col_ref[...]
    row = row_ref[0]
    mask = _causal_mask(d)
    end_row = CHUNK - 1 if d == 0 else 0

    for h in range(HEADS):
        sl = slice(h * HEAD_DIM, (h + 1) * HEAD_DIM)
        ci = d * 8 + h
        cf = d * 8 + HEADS + h
        qh = q[:, sl]
        kh = k[:, sl]
        vb = v_ref[:, sl].astype(BF16)
        qb = qh.astype(BF16)
        bc = col[:, cf:cf + 1]
        li_c = col[:, ci:ci + 1]
        br = row[cf:cf + 1, :]
        li_r = row[ci:ci + 1, :]
        b_end = bc[end_row:end_row + 1, :]
        m_prev = m_ref[h:h + 1, 0:1]
        c_old = c_ref[h]
        n_old = n_ref[0:1, sl]

        dmat = jnp.where(mask, bc - br + li_r, -jnp.inf)
        inter = bc + m_prev
        m_t = jnp.maximum(inter, jnp.max(dmat, axis=-1, keepdims=True))
        s = _dot_nt(qb, kh.astype(BF16)) * jnp.exp(dmat - m_t)
        w_inter = jnp.exp(inter - m_t)
        num = _dot(s.astype(BF16), vb) + w_inter * _dot(qb, c_old.astype(BF16))
        den = jnp.sum(s, axis=-1, keepdims=True) + w_inter * jnp.sum(qh * n_old, axis=-1, keepdims=True)
        hout = num / jnp.maximum(jnp.abs(den), jnp.exp(-m_t))

        a = b_end - bc + li_c
        m_new = jnp.maximum(b_end + m_prev, jnp.max(a, axis=0, keepdims=True))
        decay = jnp.exp(b_end + m_prev - m_new)
        kw = kh * jnp.exp(a - m_new)
        c_ref[h] = decay * c_old + _dot_tn(kw.astype(BF16), vb)
        n_ref[0:1, sl] = decay * n_old + jnp.sum(kw, axis=0, keepdims=True)
        m_ref[h:h + 1, :] = jnp.broadcast_to(m_new, (1, m_ref.shape[1]))

        if final:
            y = _sigmoid(mo_ref[:, sl]) * (hout + hprev_ref[:, sl])
            out_ref[:, sl] = _headnorm(y, ng_ref[:, sl]).astype(out_ref.dtype)
        else:
            out_ref[:, sl] = hout


def _mlstm_call(z, gcol, grow, conv_w, geom, d, hprev=None, norm_g=None):
    rows = z.shape[0]
    final = hprev is not None
    blocks8 = rows // HALO
    per = CHUNK // HALO

    def cur(colblk):
        return pl.BlockSpec((CHUNK, WIDTH), lambda b, j: (geom.chunk_block(d, b, j), colblk))

    def prev(colblk):
        return pl.BlockSpec((HALO, WIDTH), lambda b, j: (jnp.maximum(geom.chunk_block(d, b, j) * per - 1, 0), colblk))

    def nxt(colblk):
        return pl.BlockSpec((HALO, WIDTH), lambda b, j: (jnp.minimum((geom.chunk_block(d, b, j) + 1) * per, blocks8 - 1), colblk))

    in_specs = [
        prev(COL_MQ), cur(COL_MQ), nxt(COL_MQ),
        prev(COL_MK), cur(COL_MK), nxt(COL_MK),
        cur(COL_MV),
        pl.BlockSpec((CHUNK, GATE_PAD), lambda b, j: (geom.chunk_block(d, b, j), 0)),
        pl.BlockSpec((1, GATE_COLS, CHUNK), lambda b, j: (geom.chunk_block(d, b, j), 0, 0)),
        pl.BlockSpec((MOD_ROWS, 2 * WIDTH), lambda b, j: (0, 0)),
    ]
    args = [z, z, z, z, z, z, z, gcol, grow, conv_w]
    if final:
        in_specs += [
            pl.BlockSpec((CHUNK, WIDTH), lambda b, j: (geom.chunk_block(d, b, j), 0)),
            cur(COL_MO),
            pl.BlockSpec((1, WIDTH), lambda b, j: (0, 0)),
        ]
        args += [hprev, z, norm_g]
    return pl.pallas_call(
        functools.partial(_mlstm_kernel, d=d, final=final, geom=geom),
        grid=(geom.batch, geom.steps),
        in_specs=in_specs,
        out_specs=pl.BlockSpec((CHUNK, WIDTH), lambda b, j: (geom.chunk_block(d, b, j), 0)),
        out_shape=jax.ShapeDtypeStruct((rows, WIDTH), BF16 if final else F32),
        scratch_shapes=[
            pltpu.VMEM((HEADS, HEAD_DIM, HEAD_DIM), F32),
            pltpu.VMEM((8, WIDTH), F32),
            pltpu.VMEM((8, 128), F32),
            pltpu.VMEM((CHUNK + 2 * HALO, WIDTH), F32),
            pltpu.VMEM((CHUNK + 2 * HALO, WIDTH), F32),
        ],
        compiler_params=_params(("arbitrary", "arbitrary")),
        name="mlstm_bwd" if d else "mlstm_fwd",
    )(*args)


def _ret_kernel(*refs, d, final, geom):
    if final:
        (q_ref, k_ref, v_ref, cos_ref, sin_ref, lg_ref, hprev_ref, rg_ref, ng_ref,
         out_ref, r_ref, dm_ref, wc_ref, dec_ref) = refs
    else:
        (q_ref, k_ref, v_ref, cos_ref, sin_ref, lg_ref,
         out_ref, r_ref, dm_ref, wc_ref, dec_ref) = refs
    j = pl.program_id(1)

    @pl.when(j == 0)
    def _():
        r_ref[...] = jnp.zeros_like(r_ref)
        log_g = _log_sigmoid(lg_ref[...])
        t = lax.broadcasted_iota(jnp.int32, (CHUNK, CHUNK), 0)
        s = lax.broadcasted_iota(jnp.int32, (CHUNK, CHUNK), 1)
        rel = (t - s) if d == 0 else (s - t)
        relf = jnp.maximum(rel, 0).astype(F32)
        for h in range(HEADS):
            dm_ref[h] = jnp.where(rel >= 0, jnp.exp(log_g[:, h:h + 1] * relf), 0.0)
        tf = t.astype(F32)
        to_end = (CHUNK - 1.0 - tf) if d == 0 else tf
        from_start = (tf + 1.0) if d == 0 else (CHUNK - tf)
        wc_ref[...] = jnp.exp(log_g * jnp.where(s < HEADS, to_end, from_start))
        dec_ref[...] = jnp.broadcast_to(jnp.exp(log_g * float(CHUNK)), dec_ref.shape)

    is_ctx = j < geom.nctx
    cos = jnp.where(is_ctx, 1.0, cos_ref[...])
    sin = jnp.where(is_ctx, 0.0, sin_ref[...])
    half = HEAD_DIM // 2

    def rope(x_ref, h, scale):
        x1 = x_ref[:, h * HEAD_DIM:h * HEAD_DIM + half]
        x2 = x_ref[:, h * HEAD_DIM + half:(h + 1) * HEAD_DIM]
        out = jnp.concatenate([x1 * cos - x2 * sin, x2 * cos + x1 * sin], axis=-1)
        return out * scale if scale != 1.0 else out

    for h in range(HEADS):
        sl = slice(h * HEAD_DIM, (h + 1) * HEAD_DIM)
        qb = rope(q_ref, h, HEAD_DIM ** -0.5).astype(BF16)
        kh = rope(k_ref, h, 1.0)
        vb = v_ref[:, sl].astype(BF16)
        r_old = r_ref[h]
        w_end = wc_ref[:, h:h + 1]
        w_cross = wc_ref[:, HEADS + h:HEADS + h + 1]
        decay = dec_ref[0:1, h:h + 1]

        s = _dot_nt(qb, kh.astype(BF16)) * dm_ref[h]
        hout = _dot(s.astype(BF16), vb) + _dot(qb, r_old.astype(BF16)) * w_cross
        r_ref[h] = decay * r_old + _dot_tn((kh * w_end).astype(BF16), vb)

        if final:
            y = _headnorm(hout + hprev_ref[:, sl], ng_ref[:, sl]) * _silu(rg_ref[:, sl])
            out_ref[:, sl] = y.astype(out_ref.dtype)
        else:
            out_ref[:, sl] = hout


def _ret_call(z, cos, sin, decay_logit_row, geom, d, hprev=None, norm_g=None):
    rows = z.shape[0]
    final = hprev is not None

    def cur(colblk):
        return pl.BlockSpec((CHUNK, WIDTH), lambda b, j: (geom.chunk_block(d, b, j), colblk))

    def table():
        return pl.BlockSpec((CHUNK, HEAD_DIM // 2), lambda b, j: (geom.lat_chunk(d, j), 0))

    in_specs = [cur(COL_RQ), cur(COL_RK), cur(COL_RV), table(), table(),
                pl.BlockSpec((1, 128), lambda b, j: (0, 0))]
    args = [z, z, z, cos, sin, decay_logit_row]
    if final:
        in_specs += [
            pl.BlockSpec((CHUNK, WIDTH), lambda b, j: (geom.chunk_block(d, b, j), 0)),
            cur(COL_RG),
            pl.BlockSpec((1, WIDTH), lambda b, j: (0, 0)),
        ]
        args += [hprev, z, norm_g]
    return pl.pallas_call(
        functools.partial(_ret_kernel, d=d, final=final, geom=geom),
        grid=(geom.batch, geom.steps),
        in_specs=in_specs,
        out_specs=pl.BlockSpec((CHUNK, WIDTH), lambda b, j: (geom.chunk_block(d, b, j), 0)),
        out_shape=jax.ShapeDtypeStruct((rows, WIDTH), BF16 if final else F32),
        scratch_shapes=[
            pltpu.VMEM((HEADS, HEAD_DIM, HEAD_DIM), F32),
            pltpu.VMEM((HEADS, CHUNK, CHUNK), F32),
            pltpu.VMEM((CHUNK, CHUNK), F32),
            pltpu.VMEM((8, 128), F32),
        ],
        compiler_params=_params(("arbitrary", "arbitrary")),
        name="ret_bwd" if d else "ret_fwd",
    )(*args)


def _cmlp_kernel(u_ref, v_ref, w_ref, b_ref, out_ref):
    for g in range(S_GROUPS):
        sl = slice(g * S_GROUP_DIM, (g + 1) * S_GROUP_DIM)
        v = v_ref[:, sl]
        vc = v - jnp.mean(v, axis=-1, keepdims=True)
        vn = vc * lax.rsqrt(jnp.mean(vc * vc, axis=-1, keepdims=True) + NORM_EPS)
        mixed = _dot(w_ref[g], vn.astype(BF16)) + b_ref[:, g:g + 1]
        out_ref[:, sl] = (u_ref[:, sl] * mixed).astype(out_ref.dtype)


def _cmlp_call(z, s_w, s_b_t):
    rows = z.shape[0]
    return pl.pallas_call(
        _cmlp_kernel,
        grid=(rows // CHUNK,),
        in_specs=[
            pl.BlockSpec((CHUNK, WIDTH), lambda i: (i, COL_SU)),
            pl.BlockSpec((CHUNK, WIDTH), lambda i: (i, COL_SV)),
            pl.BlockSpec((S_GROUPS, CHUNK, CHUNK), lambda i: (0, 0, 0)),
            pl.BlockSpec((CHUNK, S_GROUPS), lambda i: (0, 0)),
        ],
        out_specs=pl.BlockSpec((CHUNK, WIDTH), lambda i: (i, 0)),
        out_shape=jax.ShapeDtypeStruct((rows, WIDTH), BF16),
        compiler_params=_params(("arbitrary",)),
        name="chunk_mlp",
    )(z, z, s_w, s_b_t)


def _merge_kernel(ym_ref, yr_ref, ys_ref, wm_ref, wr_ref, ws_ref, gm_ref, gr_ref, gs_ref, out_ref):
    y = (_sigmoid(gm_ref[...]) * _dot(ym_ref[...], wm_ref[...])
         + _sigmoid(gr_ref[...]) * _dot(yr_ref[...], wr_ref[...])
         + _sigmoid(gs_ref[...]) * _dot(ys_ref[...], ws_ref[...]))
    out_ref[...] = y.astype(out_ref.dtype)


def _merge_call(ym, yr, ys, wm, wr, ws, z, d_model):
    rows = ym.shape[0]
    tm, tn = ROW_TILE, 512
    gate0 = COL_GATES * WIDTH // tn
    per = d_model // tn

    def ybranch():
        return pl.BlockSpec((tm, WIDTH), lambda m, j: (m, 0))

    def wbranch():
        return pl.BlockSpec((WIDTH, tn), lambda m, j: (0, j))

    def gate(i):
        return pl.BlockSpec((tm, tn), lambda m, j: (m, gate0 + i * per + j))

    return pl.pallas_call(
        _merge_kernel,
        grid=(rows // tm, per),
        in_specs=[ybranch(), ybranch(), ybranch(), wbranch(), wbranch(), wbranch(), gate(0), gate(1), gate(2)],
        out_specs=pl.BlockSpec((tm, tn), lambda m, j: (m, j)),
        out_shape=jax.ShapeDtypeStruct((rows, d_model), BF16),
        compiler_params=_params(("arbitrary", "arbitrary")),
        name="merge",
    )(ym, yr, ys, wm, wr, ws, z, z, z)


def _outproj_kernel(y_ref, w_ref, x_ref, mod_ref, out_ref):
    out_ref[...] = x_ref[...] + mod_ref[0, 2:3, :] * _dot(y_ref[...], w_ref[...])


def _outproj_call(y, w_out, xf, mod_l, owner_of_tile):
    rows, d = xf.shape
    tm, tn = ROW_TILE, 512
    return pl.pallas_call(
        _outproj_kernel,
        grid=(rows // tm, d // tn),
        in_specs=[
            pl.BlockSpec((tm, d), lambda m, j: (m, 0)),
            pl.BlockSpec((d, tn), lambda m, j: (0, j)),
            pl.BlockSpec((tm, tn), lambda m, j: (m, j)),
            pl.BlockSpec((1, MOD_ROWS, tn), lambda m, j: (owner_of_tile(m), 0, j)),
        ],
        out_specs=pl.BlockSpec((tm, tn), lambda m, j: (m, j)),
        out_shape=jax.ShapeDtypeStruct((rows, d), F32),
        compiler_params=_params(("arbitrary", "arbitrary")),
        name="outproj",
    )(y, w_out, xf, mod_l)


def _ffn_kernel(*refs, final):
    if final:
        x_ref, mod_ref, g_ref, wg_ref, wu_ref, wd_ref, fg_ref, out_ref, h_ref, acc_ref = refs
    else:
        x_ref, mod_ref, g_ref, wg_ref, wu_ref, wd_ref, out_ref, h_ref, acc_ref = refs
    f = pl.program_id(1)

    @pl.when(f == 0)
    def _():
        h = _norm_mod(x_ref[...], g_ref[...], mod_ref[0, 3:4, :], mod_ref[0, 4:5, :])
        h_ref[...] = h.astype(BF16)
        acc_ref[...] = jnp.zeros_like(acc_ref)

    hb = h_ref[...]
    a = _silu(_dot(hb, wg_ref[...])) * _dot(hb, wu_ref[...])
    acc_ref[...] += _dot(a.astype(BF16), wd_ref[...])

    @pl.when(f == pl.num_programs(1) - 1)
    def _():
        x = x_ref[...] + mod_ref[0, 5:6, :] * acc_ref[...]
        if final:
            x = x * lax.rsqrt(jnp.mean(x * x, axis=-1, keepdims=True) + NORM_EPS) * fg_ref[...]
        out_ref[...] = x


def _ffn_call(xf, mod_l, g, wg, wu, wd, owner_of_tile, out_rows, final_g=None):
    d = xf.shape[1]
    d_ff = wg.shape[1]
    tm, tf = ROW_TILE, 512
    final = final_g is not None
    in_specs = [
        pl.BlockSpec((tm, d), lambda m, f: (m, 0)),
        pl.BlockSpec((1, MOD_ROWS, d), lambda m, f: (owner_of_tile(m), 0, 0)),
        pl.BlockSpec((1, d), lambda m, f: (0, 0)),
        pl.BlockSpec((d, tf), lambda m, f: (0, f)),
        pl.BlockSpec((d, tf), lambda m, f: (0, f)),
        pl.BlockSpec((tf, d), lambda m, f: (f, 0)),
    ]
    args = [xf, mod_l, g, wg, wu, wd]
    if final:
        in_specs.append(pl.BlockSpec((1, d), lambda m, f: (0, 0)))
        args.append(final_g)
    return pl.pallas_call(
        functools.partial(_ffn_kernel, final=final),
        grid=(out_rows // tm, d_ff // tf),
        in_specs=in_specs,
        out_specs=pl.BlockSpec((tm, d), lambda m, f: (m, 0)),
        out_shape=jax.ShapeDtypeStruct((out_rows, d), F32),
        scratch_shapes=[pltpu.VMEM((tm, d), BF16), pltpu.VMEM((tm, d), F32)],
        compiler_params=_params(("arbitrary", "arbitrary")),
        name="ffn",
    )(*args)


def _rope_tables(t):
    rows = t // GRID_W
    half = HEAD_DIM // 4
    freq = ROPE_BASE ** (-jnp.arange(half, dtype=F32) / half)
    ang_r = jnp.arange(rows, dtype=F32)[:, None] * freq
    ang_c = jnp.arange(GRID_W, dtype=F32)[:, None] * freq
    ang = jnp.concatenate([jnp.broadcast_to(ang_r[:, None, :], (rows, GRID_W, half)),
                           jnp.broadcast_to(ang_c[None, :, :], (rows, GRID_W, half))], axis=-1)
    ang = ang.reshape(rows * GRID_W, 2 * half)
    return jnp.cos(ang), jnp.sin(ang)


def kernel(x, c, ctx, c_ctx, ada_w, ada_b, norm_mix_g, norm_ffn_g, w_in, m_gate_b, m_conv_w, m_norm_g, r_decay_logit, r_norm_g, s_w, s_b, w_up_m, w_up_r, w_up_s, w_out, ffn_w_gate, ffn_w_up, ffn_w_down, final_norm_g):
    batch, t, d_model = x.shape
    ctx_len = ctx.shape[1]
    depth = ada_w.shape[0]
    lat_rows = batch * t
    assert t % ROW_TILE == 0 and (batch * ctx_len) % ROW_TILE == 0 and batch + 1 <= MOD_ROWS
    assert t % CHUNK == 0 and ctx_len % CHUNK == 0 and t % GRID_W == 0
    geom = _ScanGeom(batch, t, ctx_len)
    tiles_per_batch = t // ROW_TILE

    def owner_of_tile(m):
        return jnp.minimum(m // tiles_per_batch, batch)

    xf = jnp.concatenate([x.reshape(lat_rows, d_model), ctx.reshape(batch * ctx_len, d_model)], axis=0)

    cond = jnp.concatenate([c, c_ctx[None, :], jnp.zeros((MOD_ROWS - batch - 1, d_model), F32)], axis=0)
    mod = _ada_call(cond.T, ada_w, ada_b, batch + 1)
    mod = mod.reshape(depth, MOD_ROWS, 6, d_model)
    mod = jnp.pad(mod, ((0, 0), (0, 0), (0, MOD_ROWS - 6), (0, 0)))

    cos, sin = _rope_tables(t)

    g0 = 4 * WIDTH
    w_main = jnp.concatenate([w_in[:, :, :g0], w_in[:, :, g0 + GATE_COLS:]], axis=-1).astype(BF16)
    w_gate = jnp.pad(w_in[:, :, g0:g0 + GATE_COLS], ((0, 0), (0, 0), (0, GATE_PAD - GATE_COLS))).astype(BF16)
    gate_bias = jnp.pad(m_gate_b.reshape(depth, 1, GATE_COLS), ((0, 0), (0, 0), (0, GATE_PAD - GATE_COLS)))
    conv_w = jnp.pad(m_conv_w, ((0, 0), (0, MOD_ROWS - CONV_TAPS), (0, 0)))
    decay_rows = jnp.pad(jnp.concatenate([r_decay_logit, r_decay_logit], axis=-1),
                         ((0, 0), (0, 0), (0, 128 - 2 * HEADS)))
    s_b_t = jnp.swapaxes(s_b, 1, 2)

    for l in range(depth):
        last = l == depth - 1
        z, gates = _inproj_call(xf, mod[l], norm_mix_g[l][None, :], w_main[l], w_gate[l], owner_of_tile)

        gcol, grow = _gateprep_call(gates, gate_bias[l])
        hm = _mlstm_call(z, gcol, grow, conv_w[l], geom, 0)
        ym = _mlstm_call(z, gcol, grow, conv_w[l], geom, 1, hprev=hm, norm_g=m_norm_g[l][None, :])

        hr = _ret_call(z, cos, sin, decay_rows[l, 0][None, :], geom, 0)
        yr = _ret_call(z, cos, sin, decay_rows[l, 1][None, :], geom, 1, hprev=hr, norm_g=r_norm_g[l][None, :])

        ys = _cmlp_call(z, s_w[l].astype(BF16), s_b_t[l])

        y = _merge_call(ym, yr, ys, w_up_m[l].astype(BF16), w_up_r[l].astype(BF16), w_up_s[l].astype(BF16), z, d_model)
        xf = _outproj_call(y, w_out[l].astype(BF16), xf, mod[l], owner_of_tile)
        xf = _ffn_call(xf, mod[l], norm_ffn_g[l][None, :], ffn_w_gate[l].astype(BF16), ffn_w_up[l].astype(BF16),
                       ffn_w_down[l].astype(BF16), owner_of_tile,
                       lat_rows if last else xf.shape[0],
                       final_g=final_norm_g[None, :] if last else None)
    return xf.reshape(batch, t, d_model)
```

```python
import functools

import jax
import jax.numpy as jnp
from jax import lax
from jax.experimental import pallas as pl
from jax.experimental.pallas import tpu as pltpu

F32 = jnp.float32
BF16 = jnp.bfloat16

CHUNK = 128
GRID_W = 64
HEADS = 4
HEAD_DIM = 256
WIDTH = HEADS * HEAD_DIM
CONV_TAPS = 5
GATE_COLS = 16
GATE_PAD = 128
S_GROUPS = 8
S_GROUP_DIM = 128
ROPE_BASE = 10000.0
NORM_EPS = 1e-6
HALO = 16
PREP_ROWS = 256
MOD_ROWS = 8

ROW_TILE = 1024
ROW_SLAB = 256
VMEM_LIMIT = 56 * 1024 * 1024

COL_MQ, COL_MK, COL_MV, COL_MO, COL_RQ, COL_RK, COL_RV, COL_RG, COL_SU, COL_SV = range(10)
COL_GATES = 10


def _sigmoid(x):
    return 1.0 / (1.0 + jnp.exp(-x))


def _silu(x):
    return x / (1.0 + jnp.exp(-x))


def _log_sigmoid(x):
    return jnp.minimum(x, 0.0) - jnp.log(1.0 + jnp.exp(-jnp.abs(x)))


def _dot(a, b):
    return jnp.dot(a, b, preferred_element_type=F32)


def _dot_nt(a, b):
    return lax.dot_general(a, b, (((1,), (1,)), ((), ())), preferred_element_type=F32)


def _dot_tn(a, b):
    return lax.dot_general(a, b, (((0,), (0,)), ((), ())), preferred_element_type=F32)


def _params(sem, vmem=VMEM_LIMIT):
    return pltpu.CompilerParams(dimension_semantics=sem, vmem_limit_bytes=vmem)


def _ada_kernel(s_ref, w_ref, b_ref, o_ref, *, owners):
    s = _silu(s_ref[...])
    w = w_ref[0]
    rows = [jnp.sum(w * s[:, m:m + 1], axis=0, keepdims=True) for m in range(owners)]
    rows.append(jnp.zeros((MOD_ROWS - owners, w.shape[1]), F32))
    o_ref[0] = jnp.concatenate(rows, axis=0) + b_ref[0]


def _ada_call(cond_t, ada_w, ada_b, owners):
    depth, d, n = ada_w.shape
    tn = 512
    return pl.pallas_call(
        functools.partial(_ada_kernel, owners=owners),
        grid=(depth, n // tn),
        in_specs=[
            pl.BlockSpec((d, MOD_ROWS), lambda l, j: (0, 0)),
            pl.BlockSpec((1, d, tn), lambda l, j: (l, 0, j)),
            pl.BlockSpec((1, 1, tn), lambda l, j: (l, 0, j)),
        ],
        out_specs=pl.BlockSpec((1, MOD_ROWS, tn), lambda l, j: (l, 0, j)),
        out_shape=jax.ShapeDtypeStruct((depth, MOD_ROWS, n), F32),
        compiler_params=_params(("arbitrary", "arbitrary")),
        name="adaln",
    )(cond_t, ada_w, ada_b.reshape(depth, 1, n))


def _norm_mod(x, g, shift, scale):
    y = x * lax.rsqrt(jnp.mean(x * x, axis=-1, keepdims=True) + NORM_EPS) * g
    return y * (1.0 + scale) + shift


def _inproj_kernel(x_ref, mod_ref, g_ref, w_ref, wg_ref, z_ref, gate_ref, hn_ref):
    @pl.when(pl.program_id(1) == 0)
    def _():
        for r0 in range(0, x_ref.shape[0], ROW_SLAB):
            rows = slice(r0, r0 + ROW_SLAB)
            h = _norm_mod(x_ref[rows, :], g_ref[...], mod_ref[0, 0:1, :], mod_ref[0, 1:2, :])
            hb = h.astype(BF16)
            hn_ref[rows, :] = hb
            gate_ref[rows, :] = _dot(hb, wg_ref[...])

    z_ref[...] = _dot(hn_ref[...], w_ref[...]).astype(z_ref.dtype)


def _inproj_call(xf, mod_l, g, w_main, w_gate, owner_of_tile):
    rows, d = xf.shape
    n = w_main.shape[1]
    tm, tn = ROW_TILE, 512
    return pl.pallas_call(
        _inproj_kernel,
        grid=(rows // tm, n // tn),
        in_specs=[
            pl.BlockSpec((tm, d), lambda m, j: (m, 0)),
            pl.BlockSpec((1, MOD_ROWS, d), lambda m, j: (owner_of_tile(m), 0, 0)),
            pl.BlockSpec((1, d), lambda m, j: (0, 0)),
            pl.BlockSpec((d, tn), lambda m, j: (0, j)),
            pl.BlockSpec((d, GATE_PAD), lambda m, j: (0, 0)),
        ],
        out_specs=[
            pl.BlockSpec((tm, tn), lambda m, j: (m, j)),
            pl.BlockSpec((tm, GATE_PAD), lambda m, j: (m, 0)),
        ],
        out_shape=[
            jax.ShapeDtypeStruct((rows, n), BF16),
            jax.ShapeDtypeStruct((rows, GATE_PAD), F32),
        ],
        scratch_shapes=[pltpu.VMEM((tm, d), BF16)],
        compiler_params=_params(("arbitrary", "arbitrary")),
        name="inproj",
    )(xf, mod_l, g, w_main, w_gate)


def _qkprep_kernel(p_ref, cur_ref, nx_ref, cw_ref, out_ref, ext_ref, *, lat_tiles, lat_per_seq, ctx_per_seq):
    i = pl.program_id(0)
    is_lat = i < lat_tiles
    pos = jnp.where(is_lat, i % lat_per_seq, (i - lat_tiles) % ctx_per_seq)
    per_seq = jnp.where(is_lat, lat_per_seq, ctx_per_seq)
    keep_prev = jnp.where(pos == 0, 0.0, 1.0)
    keep_next = jnp.where(pos == per_seq - 1, 0.0, 1.0)
    ext_ref[0:HALO, :] = p_ref[...].astype(F32) * keep_prev
    ext_ref[HALO:HALO + PREP_ROWS, :] = cur_ref[...].astype(F32)
    ext_ref[HALO + PREP_ROWS:, :] = nx_ref[...].astype(F32) * keep_next
    acc = None
    for tap in range(CONV_TAPS):
        start = HALO - CONV_TAPS // 2 + tap
        term = ext_ref[start:start + PREP_ROWS, :] * cw_ref[tap:tap + 1, :]
        acc = term if acc is None else acc + term
    y = _silu(acc)
    out_ref[:, 0:WIDTH] = y[:, 0:WIDTH].astype(out_ref.dtype)
    out_ref[:, WIDTH:] = (y[:, WIDTH:] * (HEAD_DIM ** -0.5)).astype(out_ref.dtype)


def _qkprep_call(z, conv_w, seq_rows, lat_rows, lat_len, ctx_len):
    tiles = seq_rows // PREP_ROWS
    per = PREP_ROWS // HALO
    blocks = seq_rows // HALO
    lat_per_seq = lat_len // PREP_ROWS
    ctx_per_seq = ctx_len // PREP_ROWS
    lat_tiles = lat_rows // PREP_ROWS
    return pl.pallas_call(
        functools.partial(_qkprep_kernel, lat_tiles=lat_tiles, lat_per_seq=lat_per_seq, ctx_per_seq=ctx_per_seq),
        grid=(tiles,),
        in_specs=[
            pl.BlockSpec((HALO, 2 * WIDTH), lambda i: (jnp.maximum(i * per - 1, 0), 0)),
            pl.BlockSpec((PREP_ROWS, 2 * WIDTH), lambda i: (i, 0)),
            pl.BlockSpec((HALO, 2 * WIDTH), lambda i: (jnp.minimum((i + 1) * per, blocks - 1), 0)),
            pl.BlockSpec((MOD_ROWS, 2 * WIDTH), lambda i: (0, 0)),
        ],
        out_specs=pl.BlockSpec((PREP_ROWS, 2 * WIDTH), lambda i: (i, 0)),
        out_shape=jax.ShapeDtypeStruct((seq_rows, 2 * WIDTH), BF16),
        scratch_shapes=[pltpu.VMEM((PREP_ROWS + 2 * HALO, 2 * WIDTH), F32)],
        compiler_params=_params(("arbitrary",)),
        name="qkprep",
    )(z, z, z, conv_w)


def _gateprep_kernel(g_ref, b_ref, col_ref, row_ref):
    g = g_ref[...] + b_ref[...]
    lane = lax.broadcasted_iota(jnp.int32, (CHUNK, GATE_PAD), 1)
    t = lax.broadcasted_iota(jnp.int32, (CHUNK, GATE_PAD), 0)
    is_forget = (lane % 8) >= HEADS
    is_backward = lane >= 8
    lf = _log_sigmoid(g)
    pre = lf
    suf = lf
    s = 1
    while s < CHUNK:
        pre = pre + jnp.where(t >= s, pltpu.roll(pre, s, 0), 0.0)
        suf = suf + jnp.where(t < CHUNK - s, pltpu.roll(suf, CHUNK - s, 0), 0.0)
        s *= 2
    col = jnp.where(is_forget, jnp.where(is_backward, suf, pre), g)
    col_ref[...] = col
    row_ref[0] = col.T[0:GATE_COLS, :]


def _gateprep_call(gates, bias_row, seq_rows):
    nchunks = seq_rows // CHUNK
    return pl.pallas_call(
        _gateprep_kernel,
        grid=(nchunks,),
        in_specs=[
            pl.BlockSpec((CHUNK, GATE_PAD), lambda i: (i, 0)),
            pl.BlockSpec((1, GATE_PAD), lambda i: (0, 0)),
        ],
        out_specs=[
            pl.BlockSpec((CHUNK, GATE_PAD), lambda i: (i, 0)),
            pl.BlockSpec((1, GATE_COLS, CHUNK), lambda i: (i, 0, 0)),
        ],
        out_shape=[
            jax.ShapeDtypeStruct((seq_rows, GATE_PAD), F32),
            jax.ShapeDtypeStruct((nchunks, GATE_COLS, CHUNK), F32),
        ],
        compiler_params=_params(("arbitrary",)),
        name="gateprep",
    )(gates, bias_row)


class _ScanGeom:
    def __init__(self, batch, t, ctx_len):
        self.batch = batch
        self.nlat = t // CHUNK
        self.nctx = ctx_len // CHUNK
        self.steps = self.nlat + self.nctx
        self.seq_rows = batch * (t + ctx_len)

    def chunk_block(self, d, b, j):
        if d == 0:
            ctx_i, lat_i = j, j - self.nctx
        else:
            ctx_i, lat_i = self.nctx - 1 - j, self.steps - 1 - j
        return jnp.where(j < self.nctx,
                         self.batch * self.nlat + b * self.nctx + ctx_i,
                         b * self.nlat + lat_i)

    def lat_chunk(self, d, j):
        lat_i = j - self.nctx if d == 0 else self.steps - 1 - j
        return jnp.clip(lat_i, 0, self.nlat - 1)


def _headnorm(y, g):
    yc = y - jnp.mean(y, axis=-1, keepdims=True)
    return yc * lax.rsqrt(jnp.mean(yc * yc, axis=-1, keepdims=True) + NORM_EPS) * g


def _causal_mask(d):
    t = lax.broadcasted_iota(jnp.int32, (CHUNK, CHUNK), 0)
    s = lax.broadcasted_iota(jnp.int32, (CHUNK, CHUNK), 1)
    return (t >= s) if d == 0 else (s >= t)


def _mlstm_kernel(*refs, d, final):
    if final:
        (q_ref, k_ref, v_ref, col_ref, row_ref, hprev_ref, mo_ref, ng_ref, _zero_init,
         out_ref, c_ref, n_ref, m_ref) = refs
    else:
        q_ref, k_ref, v_ref, col_ref, row_ref, out_ref, c_ref, n_ref, m_ref = refs

    @pl.when(pl.program_id(1) == 0)
    def _():
        c_ref[...] = jnp.zeros_like(c_ref)
        n_ref[...] = jnp.zeros_like(n_ref)
        m_ref[...] = jnp.full(m_ref.shape, -jnp.inf, F32)

    col = col_ref[...]
    row = row_ref[0]
    mask = _causal_mask(d)
    end_row = CHUNK - 1 if d == 0 else 0

    for h in range(HEADS):
        sl = slice(h * HEAD_DIM, (h + 1) * HEAD_DIM)
        ci = d * 8 + h
        cf = d * 8 + HEADS + h
        qb = q_ref[:, sl]
        kb = k_ref[:, sl]
        vb = v_ref[:, sl]
        bc = col[:, cf:cf + 1]
        li_c = col[:, ci:ci + 1]
        br = row[cf:cf + 1, :]
        li_r = row[ci:ci + 1, :]
        b_end = bc[end_row:end_row + 1, :]
        m_prev = m_ref[h:h + 1, 0:1]
        c_old = c_ref[h]
        n_old = n_ref[0:1, sl]

        dmat = jnp.where(mask, bc - br + li_r, -jnp.inf)
        inter = bc + m_prev
        m_t = jnp.maximum(inter, jnp.max(dmat, axis=-1, keepdims=True))
        s = _dot_nt(qb, kb) * jnp.exp(dmat - m_t)
        w_inter = jnp.exp(inter - m_t)
        num = _dot(s.astype(BF16), vb) + w_inter * _dot(qb, c_old.astype(BF16))
        den = (jnp.sum(s, axis=-1, keepdims=True)
               + w_inter * jnp.sum(qb.astype(F32) * n_old, axis=-1, keepdims=True))
        hout = num / jnp.maximum(jnp.abs(den), jnp.exp(-m_t))

        a = b_end - bc + li_c
        m_new = jnp.maximum(b_end + m_prev, jnp.max(a, axis=0, keepdims=True))
        decay = jnp.exp(b_end + m_prev - m_new)
        kw = kb.astype(F32) * jnp.exp(a - m_new)
        c_ref[h] = decay * c_old + _dot_tn(kw.astype(BF16), vb)
        n_ref[0:1, sl] = decay * n_old + jnp.sum(kw, axis=0, keepdims=True)
        m_ref[h:h + 1, :] = jnp.broadcast_to(m_new, (1, m_ref.shape[1]))

        if final:
            y = _sigmoid(mo_ref[:, sl].astype(F32)) * (hout + hprev_ref[:, sl])
            out_ref[:, sl] = _headnorm(y, ng_ref[:, sl]).astype(out_ref.dtype)
        else:
            out_ref[:, sl] = hout


def _final_branch_io(in_specs, args, hprev, z_gate_spec, z, norm_g, out_rows):
    in_specs += [hprev[1], z_gate_spec, pl.BlockSpec((1, WIDTH), lambda b, j: (0, 0)),
                 pl.BlockSpec(memory_space=pl.ANY)]
    args += [hprev[0], z, norm_g, jnp.zeros((out_rows, WIDTH), BF16)]
    return {len(args) - 1: 0}


def _mlstm_call(z, qk, gcol, grow, geom, d, hprev=None, norm_g=None, out_rows=None):
    final = hprev is not None

    def cur(colblk):
        return pl.BlockSpec((CHUNK, WIDTH), lambda b, j: (geom.chunk_block(d, b, j), colblk))

    in_specs = [
        cur(0), cur(1), cur(COL_MV),
        pl.BlockSpec((CHUNK, GATE_PAD), lambda b, j: (geom.chunk_block(d, b, j), 0)),
        pl.BlockSpec((1, GATE_COLS, CHUNK), lambda b, j: (geom.chunk_block(d, b, j), 0, 0)),
    ]
    args = [qk, qk, z, gcol, grow]
    aliases = {}
    if final:
        aliases = _final_branch_io(in_specs, args, (hprev, cur(0)), cur(COL_MO), z, norm_g, out_rows)
    return pl.pallas_call(
        functools.partial(_mlstm_kernel, d=d, final=final),
        grid=(geom.batch, geom.steps),
        in_specs=in_specs,
        out_specs=cur(0),
        input_output_aliases=aliases,
        out_shape=jax.ShapeDtypeStruct((out_rows if final else geom.seq_rows, WIDTH), BF16 if final else F32),
        scratch_shapes=[
            pltpu.VMEM((HEADS, HEAD_DIM, HEAD_DIM), F32),
            pltpu.VMEM((8, WIDTH), F32),
            pltpu.VMEM((8, 128), F32),
        ],
        compiler_params=_params(("arbitrary", "arbitrary")),
        name="mlstm_bwd" if d else "mlstm_fwd",
    )(*args)


def _ret_kernel(*refs, d, final, geom):
    if final:
        (q_ref, k_ref, v_ref, cos_ref, sin_ref, lg_ref, hprev_ref, rg_ref, ng_ref, _zero_init,
         out_ref, r_ref, dm_ref, wc_ref, dec_ref) = refs
    else:
        (q_ref, k_ref, v_ref, cos_ref, sin_ref, lg_ref,
         out_ref, r_ref, dm_ref, wc_ref, dec_ref) = refs
    j = pl.program_id(1)

    @pl.when(j == 0)
    def _():
        r_ref[...] = jnp.zeros_like(r_ref)
        log_g = _log_sigmoid(lg_ref[...])
        t = lax.broadcasted_iota(jnp.int32, (CHUNK, CHUNK), 0)
        s = lax.broadcasted_iota(jnp.int32, (CHUNK, CHUNK), 1)
        rel = (t - s) if d == 0 else (s - t)
        relf = jnp.maximum(rel, 0).astype(F32)
        for h in range(HEADS):
            dm_ref[h] = jnp.where(rel >= 0, jnp.exp(log_g[:, h:h + 1] * relf), 0.0)
        tf = t.astype(F32)
        to_end = (CHUNK - 1.0 - tf) if d == 0 else tf
        from_start = (tf + 1.0) if d == 0 else (CHUNK - tf)
        wc_ref[...] = jnp.exp(log_g * jnp.where(s < HEADS, to_end, from_start))
        dec_ref[...] = jnp.broadcast_to(jnp.exp(log_g * float(CHUNK)), dec_ref.shape)

    is_ctx = j < geom.nctx
    cos = jnp.where(is_ctx, 1.0, cos_ref[...])
    sin = jnp.where(is_ctx, 0.0, sin_ref[...])
    half = HEAD_DIM // 2

    def rope(x_ref, h, scale):
        x1 = x_ref[:, h * HEAD_DIM:h * HEAD_DIM + half].astype(F32)
        x2 = x_ref[:, h * HEAD_DIM + half:(h + 1) * HEAD_DIM].astype(F32)
        out = jnp.concatenate([x1 * cos - x2 * sin, x2 * cos + x1 * sin], axis=-1)
        return out * scale if scale != 1.0 else out

    for h in range(HEADS):
        sl = slice(h * HEAD_DIM, (h + 1) * HEAD_DIM)
        qb = rope(q_ref, h, HEAD_DIM ** -0.5).astype(BF16)
        kh = rope(k_ref, h, 1.0)
        vb = v_ref[:, sl]
        r_old = r_ref[h]
        w_end = wc_ref[:, h:h + 1]
        w_cross = wc_ref[:, HEADS + h:HEADS + h + 1]
        decay = dec_ref[0:1, h:h + 1]

        s = _dot_nt(qb, kh.astype(BF16)) * dm_ref[h]
        hout = _dot(s.astype(BF16), vb) + _dot(qb, r_old.astype(BF16)) * w_cross
        r_ref[h] = decay * r_old + _dot_tn((kh * w_end).astype(BF16), vb)

        if final:
            y = _headnorm(hout + hprev_ref[:, sl], ng_ref[:, sl]) * _silu(rg_ref[:, sl].astype(F32))
            out_ref[:, sl] = y.astype(out_ref.dtype)
        else:
            out_ref[:, sl] = hout


def _ret_call(z, cos, sin, decay_logit_row, geom, d, hprev=None, norm_g=None, out_rows=None):
    final = hprev is not None

    def cur(colblk):
        return pl.BlockSpec((CHUNK, WIDTH), lambda b, j: (geom.chunk_block(d, b, j), colblk))

    def table():
        return pl.BlockSpec((CHUNK, HEAD_DIM // 2), lambda b, j: (geom.lat_chunk(d, j), 0))

    in_specs = [cur(COL_RQ), cur(COL_RK), cur(COL_RV), table(), table(),
                pl.BlockSpec((1, 128), lambda b, j: (0, 0))]
    args = [z, z, z, cos, sin, decay_logit_row]
    aliases = {}
    if final:
        aliases = _final_branch_io(in_specs, args, (hprev, cur(0)), cur(COL_RG), z, norm_g, out_rows)
    return pl.pallas_call(
        functools.partial(_ret_kernel, d=d, final=final, geom=geom),
        grid=(geom.batch, geom.steps),
        in_specs=in_specs,
        out_specs=cur(0),
        input_output_aliases=aliases,
        out_shape=jax.ShapeDtypeStruct((out_rows if final else geom.seq_rows, WIDTH), BF16 if final else F32),
        scratch_shapes=[
            pltpu.VMEM((HEADS, HEAD_DIM, HEAD_DIM), F32),
            pltpu.VMEM((HEADS, CHUNK, CHUNK), F32),
            pltpu.VMEM((CHUNK, CHUNK), F32),
            pltpu.VMEM((8, 128), F32),
        ],
        compiler_params=_params(("arbitrary", "arbitrary")),
        name="ret_bwd" if d else "ret_fwd",
    )(*args)


def _cmlp_kernel(u_ref, v_ref, w_ref, b_ref, out_ref):
    for g in range(S_GROUPS):
        sl = slice(g * S_GROUP_DIM, (g + 1) * S_GROUP_DIM)
        v = v_ref[:, sl].astype(F32)
        vc = v - jnp.mean(v, axis=-1, keepdims=True)
        vn = vc * lax.rsqrt(jnp.mean(vc * vc, axis=-1, keepdims=True) + NORM_EPS)
        mixed = _dot(w_ref[g], vn.astype(BF16)) + b_ref[:, g:g + 1]
        out_ref[:, sl] = (u_ref[:, sl].astype(F32) * mixed).astype(out_ref.dtype)


def _cmlp_call(z, s_w, s_b_t):
    seq_rows = z.shape[0]
    return pl.pallas_call(
        _cmlp_kernel,
        grid=(seq_rows // CHUNK,),
        in_specs=[
            pl.BlockSpec((CHUNK, WIDTH), lambda i: (i, COL_SU)),
            pl.BlockSpec((CHUNK, WIDTH), lambda i: (i, COL_SV)),
            pl.BlockSpec((S_GROUPS, CHUNK, CHUNK), lambda i: (0, 0, 0)),
            pl.BlockSpec((CHUNK, S_GROUPS), lambda i: (0, 0)),
        ],
        out_specs=pl.BlockSpec((CHUNK, WIDTH), lambda i: (i, 0)),
        out_shape=jax.ShapeDtypeStruct((seq_rows, WIDTH), BF16),
        compiler_params=_params(("arbitrary",)),
        name="chunk_mlp",
    )(z, z, s_w, s_b_t)


def _merge_kernel(ym_ref, yr_ref, ys_ref, wm_ref, wr_ref, ws_ref, gm_ref, gr_ref, gs_ref, out_ref):
    y = (_sigmoid(gm_ref[...].astype(F32)) * _dot(ym_ref[...], wm_ref[...])
         + _sigmoid(gr_ref[...].astype(F32)) * _dot(yr_ref[...], wr_ref[...])
         + _sigmoid(gs_ref[...].astype(F32)) * _dot(ys_ref[...], ws_ref[...]))
    out_ref[...] = y.astype(out_ref.dtype)


def _merge_call(ym, yr, ys, wm, wr, ws, z, d_model, out_rows):
    tm, tn = ROW_TILE, 512
    gate0 = COL_GATES * WIDTH // tn
    per = d_model // tn

    def ybranch():
        return pl.BlockSpec((tm, WIDTH), lambda m, j: (m, 0))

    def wbranch():
        return pl.BlockSpec((WIDTH, tn), lambda m, j: (0, j))

    def gate(i):
        return pl.BlockSpec((tm, tn), lambda m, j: (m, gate0 + i * per + j))

    return pl.pallas_call(
        _merge_kernel,
        grid=(out_rows // tm, per),
        in_specs=[ybranch(), ybranch(), ybranch(), wbranch(), wbranch(), wbranch(), gate(0), gate(1), gate(2)],
        out_specs=pl.BlockSpec((tm, tn), lambda m, j: (m, j)),
        out_shape=jax.ShapeDtypeStruct((out_rows, d_model), BF16),
        compiler_params=_params(("arbitrary", "arbitrary")),
        name="merge",
    )(ym, yr, ys, wm, wr, ws, z, z, z)


def _outproj_kernel(y_ref, w_ref, x_ref, mod_ref, out_ref):
    out_ref[...] = x_ref[...] + mod_ref[0, 2:3, :] * _dot(y_ref[...], w_ref[...])


def _outproj_call(y, w_out, xf, mod_l, owner_of_tile):
    rows, d = y.shape
    tm, tn = ROW_TILE, 512
    return pl.pallas_call(
        _outproj_kernel,
        grid=(rows // tm, d // tn),
        in_specs=[
            pl.BlockSpec((tm, d), lambda m, j: (m, 0)),
            pl.BlockSpec((d, tn), lambda m, j: (0, j)),
            pl.BlockSpec((tm, tn), lambda m, j: (m, j)),
            pl.BlockSpec((1, MOD_ROWS, tn), lambda m, j: (owner_of_tile(m), 0, j)),
        ],
        out_specs=pl.BlockSpec((tm, tn), lambda m, j: (m, j)),
        out_shape=jax.ShapeDtypeStruct((rows, d), F32),
        compiler_params=_params(("arbitrary", "arbitrary")),
        name="outproj",
    )(y, w_out, xf, mod_l)


FFN_OUT_SLAB = 512


def _ffn_kernel(*refs, final):
    if final:
        x_ref, mod_ref, g_ref, wg_ref, wu_ref, wd_ref, fg_ref, out_ref, h_ref = refs
    else:
        x_ref, mod_ref, g_ref, wg_ref, wu_ref, wd_ref, out_ref, h_ref = refs
    f = pl.program_id(1)

    @pl.when(f == 0)
    def _():
        for r0 in range(0, x_ref.shape[0], ROW_SLAB):
            rows = slice(r0, r0 + ROW_SLAB)
            h = _norm_mod(x_ref[rows, :], g_ref[...], mod_ref[0, 3:4, :], mod_ref[0, 4:5, :])
            h_ref[rows, :] = h.astype(BF16)
        out_ref[...] = jnp.zeros_like(out_ref)

    hb = h_ref[...]
    a = (_silu(_dot(hb, wg_ref[...])) * _dot(hb, wu_ref[...])).astype(BF16)
    d = out_ref.shape[1]
    for c0 in range(0, d, FFN_OUT_SLAB):
        out_ref[:, c0:c0 + FFN_OUT_SLAB] += _dot(a, wd_ref[:, c0:c0 + FFN_OUT_SLAB])

    @pl.when(f == pl.num_programs(1) - 1)
    def _():
        for r0 in range(0, x_ref.shape[0], ROW_SLAB):
            rows = slice(r0, r0 + ROW_SLAB)
            x = x_ref[rows, :] + mod_ref[0, 5:6, :] * out_ref[rows, :]
            if final:
                x = x * lax.rsqrt(jnp.mean(x * x, axis=-1, keepdims=True) + NORM_EPS) * fg_ref[...]
            out_ref[rows, :] = x


def _ffn_call(xf, mod_l, g, wg, wu, wd, owner_of_tile, out_rows, final_g=None):
    d = xf.shape[1]
    d_ff = wg.shape[1]
    tm, tf = ROW_TILE, 256
    final = final_g is not None
    in_specs = [
        pl.BlockSpec((tm, d), lambda m, f: (m, 0)),
        pl.BlockSpec((1, MOD_ROWS, d), lambda m, f: (owner_of_tile(m), 0, 0)),
        pl.BlockSpec((1, d), lambda m, f: (0, 0)),
        pl.BlockSpec((d, tf), lambda m, f: (0, f)),
        pl.BlockSpec((d, tf), lambda m, f: (0, f)),
        pl.BlockSpec((tf, d), lambda m, f: (f, 0)),
    ]
    args = [xf, mod_l, g, wg, wu, wd]
    if final:
        in_specs.append(pl.BlockSpec((1, d), lambda m, f: (0, 0)))
        args.append(final_g)
    return pl.pallas_call(
        functools.partial(_ffn_kernel, final=final),
        grid=(out_rows // tm, d_ff // tf),
        in_specs=in_specs,
        out_specs=pl.BlockSpec((tm, d), lambda m, f: (m, 0)),
        out_shape=jax.ShapeDtypeStruct((out_rows, d), F32),
        scratch_shapes=[pltpu.VMEM((tm, d), BF16)],
        compiler_params=_params(("arbitrary", "arbitrary")),
        name="ffn",
    )(*args)


def _rope_tables(t):
    rows = t // GRID_W
    half = HEAD_DIM // 4
    freq = ROPE_BASE ** (-jnp.arange(half, dtype=F32) / half)
    ang_r = jnp.arange(rows, dtype=F32)[:, None] * freq
    ang_c = jnp.arange(GRID_W, dtype=F32)[:, None] * freq
    ang = jnp.concatenate([jnp.broadcast_to(ang_r[:, None, :], (rows, GRID_W, half)),
                           jnp.broadcast_to(ang_c[None, :, :], (rows, GRID_W, half))], axis=-1)
    ang = ang.reshape(rows * GRID_W, 2 * half)
    return jnp.cos(ang), jnp.sin(ang)


def kernel(x, c, ctx, c_ctx, ada_w, ada_b, norm_mix_g, norm_ffn_g, w_in, m_gate_b, m_conv_w, m_norm_g, r_decay_logit, r_norm_g, s_w, s_b, w_up_m, w_up_r, w_up_s, w_out, ffn_w_gate, ffn_w_up, ffn_w_down, final_norm_g):
    batch, t, d_model = x.shape
    ctx_len = ctx.shape[1]
    depth = ada_w.shape[0]
    lat_rows = batch * t
    ctx_rows = batch * ctx_len
    assert t % ROW_TILE == 0 and ctx_rows <= ROW_TILE and batch + 1 <= MOD_ROWS
    assert t % PREP_ROWS == 0 and ctx_len % PREP_ROWS == 0 and t % GRID_W == 0
    geom = _ScanGeom(batch, t, ctx_len)
    seq_rows = geom.seq_rows
    tiles_per_batch = t // ROW_TILE
    all_rows = lat_rows + ROW_TILE

    def owner_of_tile(m):
        return jnp.minimum(m // tiles_per_batch, batch)

    xf = jnp.concatenate([x.reshape(lat_rows, d_model), ctx.reshape(ctx_rows, d_model),
                          jnp.zeros((ROW_TILE - ctx_rows, d_model), F32)], axis=0)

    cond = jnp.concatenate([c, c_ctx[None, :], jnp.zeros((MOD_ROWS - batch - 1, d_model), F32)], axis=0)
    mod = _ada_call(cond.T, ada_w, ada_b, batch + 1)
    mod = mod.reshape(depth, MOD_ROWS, 6, d_model)
    mod = jnp.pad(mod, ((0, 0), (0, 0), (0, MOD_ROWS - 6), (0, 0)))

    cos, sin = _rope_tables(t)

    g0 = 4 * WIDTH
    w_main = jnp.concatenate([w_in[:, :, :g0], w_in[:, :, g0 + GATE_COLS:]], axis=-1).astype(BF16)
    w_gate = jnp.pad(w_in[:, :, g0:g0 + GATE_COLS], ((0, 0), (0, 0), (0, GATE_PAD - GATE_COLS))).astype(BF16)
    gate_bias = jnp.pad(m_gate_b.reshape(depth, 1, GATE_COLS), ((0, 0), (0, 0), (0, GATE_PAD - GATE_COLS)))
    conv_w = jnp.pad(m_conv_w, ((0, 0), (0, MOD_ROWS - CONV_TAPS), (0, 0)))
    decay_rows = jnp.pad(jnp.concatenate([r_decay_logit, r_decay_logit], axis=-1),
                         ((0, 0), (0, 0), (0, 128 - 2 * HEADS)))
    s_b_t = jnp.swapaxes(s_b, 1, 2)

    for l in range(depth):
        last = l == depth - 1
        z, gates = _inproj_call(xf, mod[l], norm_mix_g[l][None, :], w_main[l], w_gate[l], owner_of_tile)

        qk = _qkprep_call(z, conv_w[l], seq_rows, lat_rows, t, ctx_len)
        gcol, grow = _gateprep_call(gates, gate_bias[l], seq_rows)
        hm = _mlstm_call(z, qk, gcol, grow, geom, 0)
        ym = _mlstm_call(z, qk, gcol, grow, geom, 1, hprev=hm, norm_g=m_norm_g[l][None, :], out_rows=all_rows)

        hr = _ret_call(z, cos, sin, decay_rows[l, 0][None, :], geom, 0)
        yr = _ret_call(z, cos, sin, decay_rows[l, 1][None, :], geom, 1, hprev=hr, norm_g=r_norm_g[l][None, :],
                       out_rows=all_rows)

        ys = _cmlp_call(z, s_w[l].astype(BF16), s_b_t[l])

        rows_out = lat_rows if last else all_rows
        y = _merge_call(ym, yr, ys, w_up_m[l].astype(BF16), w_up_r[l].astype(BF16), w_up_s[l].astype(BF16),
                        z, d_model, rows_out)
        xf = _outproj_call(y, w_out[l].astype(BF16), xf, mod[l], owner_of_tile)
        xf = _ffn_call(xf, mod[l], norm_ffn_g[l][None, :], ffn_w_gate[l].astype(BF16), ffn_w_up[l].astype(BF16),
                       ffn_w_down[l].astype(BF16), owner_of_tile, rows_out,
                       final_g=final_norm_g[None, :] if last else None)
    return xf.reshape(batch, t, d_model)
```

```python
import functools

import jax
import jax.numpy as jnp
from jax import lax
from jax.experimental import pallas as pl
from jax.experimental.pallas import tpu as pltpu

F32 = jnp.float32
BF16 = jnp.bfloat16

CHUNK = 128
GRID_W = 64
HEADS = 4
HEAD_DIM = 256
WIDTH = HEADS * HEAD_DIM
CONV_TAPS = 5
GATE_COLS = 16
GATE_PAD = 128
NORM_TILE = 128
GATE_ROWS = 32
S_GROUPS = 8
S_GROUP_DIM = 128
ROPE_BASE = 10000.0
NORM_EPS = 1e-6
HALO = 16
PREP_ROWS = 256
MOD_ROWS = 8

ROW_TILE = 1024
ROW_SLAB = 256
VMEM_LIMIT = 56 * 1024 * 1024

COL_MQ, COL_MK, COL_MV, COL_MO, COL_RQ, COL_RK, COL_RV, COL_RG, COL_SU, COL_SV = range(10)
COL_GATES = 10


def _sigmoid(x):
    return 1.0 / (1.0 + jnp.exp(-x))


def _silu(x):
    return x / (1.0 + jnp.exp(-x))


def _log_sigmoid(x):
    return jnp.minimum(x, 0.0) - jnp.log(1.0 + jnp.exp(-jnp.abs(x)))


def _dot(a, b):
    return jnp.dot(a, b, preferred_element_type=F32)


def _params(sem, vmem=VMEM_LIMIT):
    return pltpu.CompilerParams(dimension_semantics=sem, vmem_limit_bytes=vmem)


def _ada_kernel(s_ref, w_ref, b_ref, o_ref, *, owners):
    s = _silu(s_ref[...])
    w = w_ref[0]
    rows = [jnp.sum(w * s[:, m:m + 1], axis=0, keepdims=True) for m in range(owners)]
    rows.append(jnp.zeros((MOD_ROWS - owners, w.shape[1]), F32))
    o_ref[0] = jnp.concatenate(rows, axis=0) + b_ref[0]


def _ada_call(cond_t, ada_w, ada_b, owners):
    depth, d, n = ada_w.shape
    tn = 512
    return pl.pallas_call(
        functools.partial(_ada_kernel, owners=owners),
        grid=(depth, n // tn),
        in_specs=[
            pl.BlockSpec((d, MOD_ROWS), lambda l, j: (0, 0)),
            pl.BlockSpec((1, d, tn), lambda l, j: (l, 0, j)),
            pl.BlockSpec((1, 1, tn), lambda l, j: (l, 0, j)),
        ],
        out_specs=pl.BlockSpec((1, MOD_ROWS, tn), lambda l, j: (l, 0, j)),
        out_shape=jax.ShapeDtypeStruct((depth, MOD_ROWS, n), F32),
        compiler_params=_params(("arbitrary", "arbitrary")),
        name="adaln",
    )(cond_t, ada_w, ada_b.reshape(depth, 1, n))


def _norm_mod(x, g, shift, scale):
    y = x * lax.rsqrt(jnp.mean(x * x, axis=-1, keepdims=True) + NORM_EPS) * g
    return y * (1.0 + scale) + shift


def _inproj_kernel(x_ref, mod_ref, g_ref, w_ref, wg_ref, z_ref, gate_ref, hn_ref):
    @pl.when(pl.program_id(1) == 0)
    def _():
        for r0 in range(0, x_ref.shape[0], ROW_SLAB):
            rows = slice(r0, r0 + ROW_SLAB)
            h = _norm_mod(x_ref[rows, :], g_ref[...], mod_ref[0, 0:1, :], mod_ref[0, 1:2, :])
            hb = h.astype(BF16)
            hn_ref[rows, :] = hb
            gate_ref[rows, :] = _dot(hb, wg_ref[...])

    z_ref[...] = _dot(hn_ref[...], w_ref[...]).astype(z_ref.dtype)


def _inproj_call(xf, mod_l, g, w_main, w_gate, owner_of_tile):
    rows, d = xf.shape
    n = w_main.shape[1]
    tm, tn = ROW_TILE, 512
    return pl.pallas_call(
        _inproj_kernel,
        grid=(rows // tm, n // tn),
        in_specs=[
            pl.BlockSpec((tm, d), lambda m, j: (m, 0)),
            pl.BlockSpec((1, MOD_ROWS, d), lambda m, j: (owner_of_tile(m), 0, 0)),
            pl.BlockSpec((1, d), lambda m, j: (0, 0)),
            pl.BlockSpec((d, tn), lambda m, j: (0, j)),
            pl.BlockSpec((d, GATE_PAD), lambda m, j: (0, 0)),
        ],
        out_specs=[
            pl.BlockSpec((tm, tn), lambda m, j: (m, j)),
            pl.BlockSpec((tm, GATE_PAD), lambda m, j: (m, 0)),
        ],
        out_shape=[
            jax.ShapeDtypeStruct((rows, n), BF16),
            jax.ShapeDtypeStruct((rows, GATE_PAD), F32),
        ],
        scratch_shapes=[pltpu.VMEM((tm, d), BF16)],
        compiler_params=_params(("arbitrary", "arbitrary")),
        name="inproj",
    )(xf, mod_l, g, w_main, w_gate)


def _transpose_chunks(k, kt_ref):
    for c in range(PREP_ROWS // CHUNK):
        for h in range(HEADS):
            blk = k[c * CHUNK:(c + 1) * CHUNK, h * HEAD_DIM:(h + 1) * HEAD_DIM]
            kt_ref[c, h * HEAD_DIM:(h + 1) * HEAD_DIM, :] = blk.T.astype(kt_ref.dtype)


def _prep_kernel(p_ref, cur_ref, nx_ref, cw_ref, r_ref, cos_ref, sin_ref,
                 mq_ref, mkt_ref, rq_ref, rkt_ref, ext_ref, *, lat_tiles, lat_per_seq, ctx_per_seq):
    i = pl.program_id(0)
    is_lat = i < lat_tiles
    pos = jnp.where(is_lat, i % lat_per_seq, (i - lat_tiles) % ctx_per_seq)
    per_seq = jnp.where(is_lat, lat_per_seq, ctx_per_seq)
    halo_zero = jnp.zeros(p_ref.shape, p_ref.dtype)
    ext_ref[0:HALO, :] = jnp.where(pos == 0, halo_zero, p_ref[...])
    ext_ref[HALO:HALO + PREP_ROWS, :] = cur_ref[...]
    ext_ref[HALO + PREP_ROWS:, :] = jnp.where(pos == per_seq - 1, halo_zero, nx_ref[...])

    window = CHUNK + 2 * HALO
    out_row = lax.broadcasted_iota(jnp.int32, (CHUNK, window), 0)
    in_row = lax.broadcasted_iota(jnp.int32, (CHUNK, window), 1)
    centre = CONV_TAPS // 2
    side_taps = [tap for tap in range(CONV_TAPS) if tap != centre]
    select = jnp.concatenate([(in_row == out_row + HALO + tap - centre).astype(BF16) for tap in side_taps], axis=0)
    for half in range(2):
        cols = slice(half * WIDTH, (half + 1) * WIDTH)
        blocks = []
        for r0 in range(0, PREP_ROWS, CHUNK):
            shifted = _dot(select, ext_ref[r0:r0 + window, cols])
            acc = ext_ref[r0 + HALO:r0 + HALO + CHUNK, cols].astype(F32) * cw_ref[centre:centre + 1, cols]
            for n, tap in enumerate(side_taps):
                acc = acc + shifted[n * CHUNK:(n + 1) * CHUNK, :] * cw_ref[tap:tap + 1, cols]
            blocks.append(_silu(acc))
        y = jnp.concatenate(blocks, axis=0)
        if half == 0:
            mq_ref[...] = y.astype(mq_ref.dtype)
        else:
            _transpose_chunks(y * (HEAD_DIM ** -0.5), mkt_ref)

    cos = jnp.where(is_lat, cos_ref[...], 1.0)
    sin = jnp.where(is_lat, sin_ref[...], 0.0)
    half_dim = HEAD_DIM // 2
    for part in range(2):
        pieces = []
        for h in range(HEADS):
            c0 = part * WIDTH + h * HEAD_DIM
            x1 = r_ref[:, c0:c0 + half_dim].astype(F32)
            x2 = r_ref[:, c0 + half_dim:c0 + HEAD_DIM].astype(F32)
            pieces += [x1 * cos - x2 * sin, x2 * cos + x1 * sin]
        rot = jnp.concatenate(pieces, axis=-1)
        if part == 0:
            rq_ref[...] = (rot * (HEAD_DIM ** -0.5)).astype(rq_ref.dtype)
        else:
            _transpose_chunks(rot, rkt_ref)


def _prep_call(z, conv_w, cos, sin, seq_rows, lat_rows, lat_len, ctx_len):
    tiles = seq_rows // PREP_ROWS
    per = PREP_ROWS // HALO
    blocks = seq_rows // HALO
    lat_per_seq = lat_len // PREP_ROWS
    ctx_per_seq = ctx_len // PREP_ROWS
    lat_tiles = lat_rows // PREP_ROWS
    cpt = PREP_ROWS // CHUNK

    def table():
        return pl.BlockSpec((PREP_ROWS, HEAD_DIM // 2), lambda i: (jnp.where(i < lat_tiles, i % lat_per_seq, 0), 0))

    def q_out():
        return pl.BlockSpec((PREP_ROWS, WIDTH), lambda i: (i, 0))

    def kt_out():
        return pl.BlockSpec((cpt, WIDTH, CHUNK), lambda i: (i, 0, 0))

    q_shape = jax.ShapeDtypeStruct((seq_rows, WIDTH), BF16)
    kt_shape = jax.ShapeDtypeStruct((seq_rows // CHUNK, WIDTH, CHUNK), BF16)
    return pl.pallas_call(
        functools.partial(_prep_kernel, lat_tiles=lat_tiles, lat_per_seq=lat_per_seq, ctx_per_seq=ctx_per_seq),
        grid=(tiles,),
        in_specs=[
            pl.BlockSpec((HALO, 2 * WIDTH), lambda i: (jnp.maximum(i * per - 1, 0), 0)),
            pl.BlockSpec((PREP_ROWS, 2 * WIDTH), lambda i: (i, 0)),
            pl.BlockSpec((HALO, 2 * WIDTH), lambda i: (jnp.minimum((i + 1) * per, blocks - 1), 0)),
            pl.BlockSpec((MOD_ROWS, 2 * WIDTH), lambda i: (0, 0)),
            pl.BlockSpec((PREP_ROWS, 2 * WIDTH), lambda i: (i, COL_RQ // 2)),
            table(), table(),
        ],
        out_specs=[q_out(), kt_out(), q_out(), kt_out()],
        out_shape=[q_shape, kt_shape, q_shape, kt_shape],
        scratch_shapes=[pltpu.VMEM((PREP_ROWS + 2 * HALO, 2 * WIDTH), BF16)],
        compiler_params=_params(("arbitrary",)),
        name="qkprep",
    )(z, z, z, conv_w, z, cos, sin)


def _gateprep_kernel(g_ref, b_ref, col_ref, row_ref):
    g = g_ref[...] + b_ref[...]
    lane = lax.broadcasted_iota(jnp.int32, (CHUNK, GATE_PAD), 1)
    t = lax.broadcasted_iota(jnp.int32, (CHUNK, GATE_PAD), 0)
    is_forget = (lane % 8) >= HEADS
    is_backward = lane >= 8
    lf = _log_sigmoid(g)
    pre = lf
    suf = lf
    s = 1
    while s < CHUNK:
        pre = pre + jnp.where(t >= s, pltpu.roll(pre, s, 0), 0.0)
        suf = suf + jnp.where(t < CHUNK - s, pltpu.roll(suf, CHUNK - s, 0), 0.0)
        s *= 2
    col = jnp.where(is_forget, jnp.where(is_backward, suf, pre), g)

    cum_f = pltpu.roll(col, GATE_PAD - HEADS, 1)
    diff = col - cum_f
    pmax = diff
    smax = diff
    s = 1
    while s < CHUNK:
        pmax = jnp.maximum(pmax, jnp.where(t >= s, pltpu.roll(pmax, s, 0), -jnp.inf))
        smax = jnp.maximum(smax, jnp.where(t < CHUNK - s, pltpu.roll(smax, CHUNK - s, 0), -jnp.inf))
        s *= 2
    dmax = cum_f + jnp.where(is_backward, smax, pmax)
    col = jnp.where(lane < GATE_COLS, col, pltpu.roll(dmax, GATE_COLS, 1))
    col_ref[...] = col

    col_t = col.T
    row_id = lax.broadcasted_iota(jnp.int32, (CHUNK, CHUNK), 0)
    end_val = jnp.where((row_id % GATE_COLS) >= 8, col_t[:, 0:1], col_t[:, CHUNK - 1:CHUNK])
    row_ref[0, 0:GATE_ROWS, :] = col_t[0:GATE_ROWS, :]
    row_ref[0, GATE_ROWS:, :] = jnp.broadcast_to(end_val, (CHUNK, CHUNK))[0:GATE_ROWS, :]


def _gateprep_call(gates, bias_row, seq_rows):
    nchunks = seq_rows // CHUNK
    return pl.pallas_call(
        _gateprep_kernel,
        grid=(nchunks,),
        in_specs=[
            pl.BlockSpec((CHUNK, GATE_PAD), lambda i: (i, 0)),
            pl.BlockSpec((1, GATE_PAD), lambda i: (0, 0)),
        ],
        out_specs=[
            pl.BlockSpec((CHUNK, GATE_PAD), lambda i: (i, 0)),
            pl.BlockSpec((1, 2 * GATE_ROWS, CHUNK), lambda i: (i, 0, 0)),
        ],
        out_shape=[
            jax.ShapeDtypeStruct((seq_rows, GATE_PAD), F32),
            jax.ShapeDtypeStruct((nchunks, 2 * GATE_ROWS, CHUNK), F32),
        ],
        compiler_params=_params(("arbitrary",)),
        name="gateprep",
    )(gates, bias_row)


class _ScanGeom:
    def __init__(self, batch, t, ctx_len):
        self.batch = batch
        self.nlat = t // CHUNK
        self.nctx = ctx_len // CHUNK
        self.steps = self.nlat + self.nctx
        self.seq_rows = batch * (t + ctx_len)

    def chunk_block(self, d, b, j):
        if d == 0:
            ctx_i, lat_i = j, j - self.nctx
        else:
            ctx_i, lat_i = self.nctx - 1 - j, self.steps - 1 - j
        return jnp.where(j < self.nctx,
                         self.batch * self.nlat + b * self.nctx + ctx_i,
                         b * self.nlat + lat_i)


def _headnorm(y, g):
    yc = y - jnp.mean(y, axis=-1, keepdims=True)
    return yc * lax.rsqrt(jnp.mean(yc * yc, axis=-1, keepdims=True) + NORM_EPS) * g


def _causal_mask(d):
    t = lax.broadcasted_iota(jnp.int32, (CHUNK, CHUNK), 0)
    s = lax.broadcasted_iota(jnp.int32, (CHUNK, CHUNK), 1)
    return (t >= s) if d == 0 else (s >= t)


def _mlstm_kernel(*refs, d, final):
    if final:
        (q_ref, kt_ref, v_ref, col_ref, row_ref, hprev_ref, mo_ref, ng_ref, _zero_init,
         out_ref, c_ref, m_ref) = refs
    else:
        q_ref, kt_ref, v_ref, col_ref, row_ref, out_ref, c_ref, m_ref = refs

    @pl.when(pl.program_id(1) == 0)
    def _():
        c_ref[...] = jnp.zeros_like(c_ref)
        m_ref[...] = jnp.full(m_ref.shape, -jnp.inf, F32)

    col = col_ref[...]
    row = row_ref[0]
    mask = _causal_mask(d)
    heads = range(HEADS)
    sls = [slice(h * HEAD_DIM, (h + 1) * HEAD_DIM) for h in heads]

    ones_col = (lax.broadcasted_iota(jnp.int32, (CHUNK, NORM_TILE), 1) == 0).astype(BF16)
    v_ext = [jnp.concatenate([v_ref[:, sls[h]], ones_col], axis=1) for h in heads]

    c_old = [c_ref[h] for h in heads]
    qk = [_dot(q_ref[:, sls[h]], kt_ref[0, sls[h], :]) for h in heads]
    qc = [_dot(q_ref[:, sls[h]], c_old[h].astype(BF16)) for h in heads]

    i_idx = [d * 8 + h for h in heads]
    f_idx = [d * 8 + HEADS + h for h in heads]
    bc = [col[:, f_idx[h]:f_idx[h] + 1] for h in heads]
    dmax = [col[:, GATE_COLS + i_idx[h]:GATE_COLS + i_idx[h] + 1] for h in heads]
    li_r = [row[i_idx[h]:i_idx[h] + 1, :] for h in heads]
    br = [row[f_idx[h]:f_idx[h] + 1, :] for h in heads]
    b_end = [row[GATE_ROWS + f_idx[h]:GATE_ROWS + f_idx[h] + 1, :] for h in heads]
    a_max = [row[GATE_ROWS + GATE_COLS + i_idx[h]:GATE_ROWS + GATE_COLS + i_idx[h] + 1, :] for h in heads]
    m_prev = [m_ref[h:h + 1, :] for h in heads]
    inter = [bc[h] + m_prev[h][:, 0:1] for h in heads]
    m_t = [jnp.maximum(inter[h], dmax[h]) for h in heads]
    w_inter = [jnp.exp(inter[h] - m_t[h]) for h in heads]
    floor = [jnp.exp(-m_t[h]) for h in heads]
    decay_mat = [jnp.exp(jnp.where(mask, (bc[h] - m_t[h]) + (li_r[h] - br[h]), -jnp.inf)) for h in heads]
    m_new = [jnp.maximum(b_end[h] + m_prev[h], a_max[h]) for h in heads]
    decay = [jnp.exp(b_end[h] + m_prev[h] - m_new[h]) for h in heads]
    w_row = [jnp.exp(b_end[h] - br[h] + li_r[h] - m_new[h]) for h in heads]
    kw_t = [(kt_ref[0, sls[h], :].astype(F32) * w_row[h]).astype(BF16) for h in heads]

    s = [(qk[h] * decay_mat[h]).astype(BF16) for h in heads]
    sv = [_dot(s[h], v_ext[h]) for h in heads]
    kv = [_dot(kw_t[h], v_ext[h]) for h in heads]
    tot = [sv[h] + w_inter[h] * qc[h] for h in heads]
    scale = [1.0 / jnp.maximum(jnp.abs(tot[h][:, HEAD_DIM:HEAD_DIM + 1]), floor[h]) for h in heads]
    hout = [tot[h][:, 0:HEAD_DIM] * scale[h] for h in heads]

    for h in heads:
        decay_ext = jnp.concatenate([decay[h]] * (c_ref.shape[2] // CHUNK), axis=1)
        c_ref[h] = decay_ext * c_old[h] + kv[h]
        m_ref[h:h + 1, :] = m_new[h]
    for h in heads:
        if final:
            y = _sigmoid(mo_ref[:, sls[h]].astype(F32)) * (hout[h] + hprev_ref[:, sls[h]])
            out_ref[:, sls[h]] = _headnorm(y, ng_ref[:, sls[h]]).astype(out_ref.dtype)
        else:
            out_ref[:, sls[h]] = hout[h]


def _final_branch_io(in_specs, args, hprev, z_gate_spec, z, norm_g, out_rows):
    in_specs += [hprev[1], z_gate_spec, pl.BlockSpec((1, WIDTH), lambda b, j: (0, 0)),
                 pl.BlockSpec(memory_space=pl.ANY)]
    args += [hprev[0], z, norm_g, jnp.zeros((out_rows, WIDTH), BF16)]
    return {len(args) - 1: 0}


def _mlstm_call(z, q, kt, gcol, grow, geom, d, hprev=None, norm_g=None, out_rows=None):
    final = hprev is not None

    def cur(colblk):
        return pl.BlockSpec((CHUNK, WIDTH), lambda b, j: (geom.chunk_block(d, b, j), colblk))

    in_specs = [
        cur(0),
        pl.BlockSpec((1, WIDTH, CHUNK), lambda b, j: (geom.chunk_block(d, b, j), 0, 0)),
        cur(COL_MV),
        pl.BlockSpec((CHUNK, GATE_PAD), lambda b, j: (geom.chunk_block(d, b, j), 0)),
        pl.BlockSpec((1, 2 * GATE_ROWS, CHUNK), lambda b, j: (geom.chunk_block(d, b, j), 0, 0)),
    ]
    args = [q, kt, z, gcol, grow]
    aliases = {}
    if final:
        aliases = _final_branch_io(in_specs, args, (hprev, cur(0)), cur(COL_MO), z, norm_g, out_rows)
    return pl.pallas_call(
        functools.partial(_mlstm_kernel, d=d, final=final),
        grid=(geom.batch, geom.steps),
        in_specs=in_specs,
        out_specs=cur(0),
        input_output_aliases=aliases,
        out_shape=jax.ShapeDtypeStruct((out_rows if final else geom.seq_rows, WIDTH), BF16 if final else F32),
        scratch_shapes=[
            pltpu.VMEM((HEADS, HEAD_DIM, HEAD_DIM + NORM_TILE), F32),
            pltpu.VMEM((8, CHUNK), F32),
        ],
        compiler_params=_params(("arbitrary", "arbitrary")),
        name="mlstm_bwd" if d else "mlstm_fwd",
    )(*args)


def _ret_kernel(*refs, d, final):
    if final:
        (q_ref, kt_ref, v_ref, lg_ref, hprev_ref, rg_ref, ng_ref, _zero_init,
         out_ref, r_ref, dm_ref, wx_ref, we_ref, dec_ref) = refs
    else:
        q_ref, kt_ref, v_ref, lg_ref, out_ref, r_ref, dm_ref, wx_ref, we_ref, dec_ref = refs

    @pl.when(pl.program_id(1) == 0)
    def _():
        r_ref[...] = jnp.zeros_like(r_ref)
        log_g = _log_sigmoid(lg_ref[...])
        t = lax.broadcasted_iota(jnp.int32, (CHUNK, CHUNK), 0)
        s = lax.broadcasted_iota(jnp.int32, (CHUNK, CHUNK), 1)
        rel = (t - s) if d == 0 else (s - t)
        relf = jnp.maximum(rel, 0).astype(F32)
        tf = t.astype(F32)
        sf = s[0:1, :].astype(F32)
        from_start = (tf + 1.0) if d == 0 else (CHUNK - tf)
        to_end = (CHUNK - 1.0 - sf) if d == 0 else sf
        for h in range(HEADS):
            lg = log_g[:, h:h + 1]
            dm_ref[h] = jnp.where(rel >= 0, jnp.exp(lg * relf), 0.0)
            wx_ref[h] = jnp.exp(lg * from_start)
            we_ref[h:h + 1, :] = jnp.exp(lg * to_end)
            dec_ref[h:h + 1, :] = jnp.broadcast_to(jnp.exp(lg * float(CHUNK)), (1, CHUNK))

    heads = range(HEADS)
    sls = [slice(h * HEAD_DIM, (h + 1) * HEAD_DIM) for h in heads]
    reps = HEAD_DIM // CHUNK

    r_old = [r_ref[h] for h in heads]
    qk = [_dot(q_ref[:, sls[h]], kt_ref[0, sls[h], :]) for h in heads]
    qr = [_dot(q_ref[:, sls[h]], r_old[h].astype(BF16)) for h in heads]
    kw_t = [(kt_ref[0, sls[h], :].astype(F32) * we_ref[h:h + 1, :]).astype(BF16) for h in heads]
    s = [(qk[h] * dm_ref[h]).astype(BF16) for h in heads]
    sv = [_dot(s[h], v_ref[:, sls[h]]) for h in heads]
    kv = [_dot(kw_t[h], v_ref[:, sls[h]]) for h in heads]
    hout = [sv[h] + qr[h] * jnp.concatenate([wx_ref[h]] * reps, axis=1) for h in heads]
    for h in heads:
        r_ref[h] = jnp.concatenate([dec_ref[h:h + 1, :]] * reps, axis=1) * r_old[h] + kv[h]
    for h in heads:
        if final:
            y = _headnorm(hout[h] + hprev_ref[:, sls[h]], ng_ref[:, sls[h]]) * _silu(rg_ref[:, sls[h]].astype(F32))
            out_ref[:, sls[h]] = y.astype(out_ref.dtype)
        else:
            out_ref[:, sls[h]] = hout[h]


def _ret_call(z, q, kt, decay_logit_row, geom, d, hprev=None, norm_g=None, out_rows=None):
    final = hprev is not None

    def cur(colblk):
        return pl.BlockSpec((CHUNK, WIDTH), lambda b, j: (geom.chunk_block(d, b, j), colblk))

    in_specs = [cur(0),
                pl.BlockSpec((1, WIDTH, CHUNK), lambda b, j: (geom.chunk_block(d, b, j), 0, 0)),
                cur(COL_RV),
                pl.BlockSpec((1, 128), lambda b, j: (0, 0))]
    args = [q, kt, z, decay_logit_row]
    aliases = {}
    if final:
        aliases = _final_branch_io(in_specs, args, (hprev, cur(0)), cur(COL_RG), z, norm_g, out_rows)
    return pl.pallas_call(
        functools.partial(_ret_kernel, d=d, final=final),
        grid=(geom.batch, geom.steps),
        in_specs=in_specs,
        out_specs=cur(0),
        input_output_aliases=aliases,
        out_shape=jax.ShapeDtypeStruct((out_rows if final else geom.seq_rows, WIDTH), BF16 if final else F32),
        scratch_shapes=[
            pltpu.VMEM((HEADS, HEAD_DIM, HEAD_DIM), F32),
            pltpu.VMEM((HEADS, CHUNK, CHUNK), F32),
            pltpu.VMEM((HEADS, CHUNK, CHUNK), F32),
            pltpu.VMEM((8, CHUNK), F32),
            pltpu.VMEM((8, CHUNK), F32),
        ],
        compiler_params=_params(("arbitrary", "arbitrary")),
        name="ret_bwd" if d else "ret_fwd",
    )(*args)


def _cmlp_kernel(u_ref, v_ref, w_ref, b_ref, out_ref):
    groups = range(S_GROUPS)
    sls = [slice(g * S_GROUP_DIM, (g + 1) * S_GROUP_DIM) for g in groups]
    v = [v_ref[:, sls[g]].astype(F32) for g in groups]
    vc = [v[g] - jnp.mean(v[g], axis=-1, keepdims=True) for g in groups]
    var = [jnp.mean(vc[g] * vc[g], axis=-1, keepdims=True) for g in groups]
    vn = [(vc[g] * lax.rsqrt(var[g] + NORM_EPS)).astype(BF16) for g in groups]
    mixed = [_dot(w_ref[g], vn[g]) + b_ref[:, g:g + 1] for g in groups]
    for g in groups:
        out_ref[:, sls[g]] = (u_ref[:, sls[g]].astype(F32) * mixed[g]).astype(out_ref.dtype)


def _cmlp_call(z, s_w, s_b_t):
    seq_rows = z.shape[0]
    return pl.pallas_call(
        _cmlp_kernel,
        grid=(seq_rows // CHUNK,),
        in_specs=[
            pl.BlockSpec((CHUNK, WIDTH), lambda i: (i, COL_SU)),
            pl.BlockSpec((CHUNK, WIDTH), lambda i: (i, COL_SV)),
            pl.BlockSpec((S_GROUPS, CHUNK, CHUNK), lambda i: (0, 0, 0)),
            pl.BlockSpec((CHUNK, S_GROUPS), lambda i: (0, 0)),
        ],
        out_specs=pl.BlockSpec((CHUNK, WIDTH), lambda i: (i, 0)),
        out_shape=jax.ShapeDtypeStruct((seq_rows, WIDTH), BF16),
        compiler_params=_params(("arbitrary",)),
        name="chunk_mlp",
    )(z, z, s_w, s_b_t)


def _merge_kernel(ym_ref, yr_ref, ys_ref, wm_ref, wr_ref, ws_ref, gm_ref, gr_ref, gs_ref, out_ref):
    y = (_sigmoid(gm_ref[...].astype(F32)) * _dot(ym_ref[...], wm_ref[...])
         + _sigmoid(gr_ref[...].astype(F32)) * _dot(yr_ref[...], wr_ref[...])
         + _sigmoid(gs_ref[...].astype(F32)) * _dot(ys_ref[...], ws_ref[...]))
    out_ref[...] = y.astype(out_ref.dtype)


def _merge_call(ym, yr, ys, wm, wr, ws, z, d_model, out_rows):
    tm, tn = ROW_TILE, 512
    gate0 = COL_GATES * WIDTH // tn
    per = d_model // tn

    def ybranch():
        return pl.BlockSpec((tm, WIDTH), lambda m, j: (m, 0))

    def wbranch():
        return pl.BlockSpec((WIDTH, tn), lambda m, j: (0, j))

    def gate(i):
        return pl.BlockSpec((tm, tn), lambda m, j: (m, gate0 + i * per + j))

    return pl.pallas_call(
        _merge_kernel,
        grid=(out_rows // tm, per),
        in_specs=[ybranch(), ybranch(), ybranch(), wbranch(), wbranch(), wbranch(), gate(0), gate(1), gate(2)],
        out_specs=pl.BlockSpec((tm, tn), lambda m, j: (m, j)),
        out_shape=jax.ShapeDtypeStruct((out_rows, d_model), BF16),
        compiler_params=_params(("arbitrary", "arbitrary")),
        name="merge",
    )(ym, yr, ys, wm, wr, ws, z, z, z)


def _outproj_kernel(y_ref, w_ref, x_ref, mod_ref, out_ref):
    out_ref[...] = x_ref[...] + mod_ref[0, 2:3, :] * _dot(y_ref[...], w_ref[...])


def _outproj_call(y, w_out, xf, mod_l, owner_of_tile):
    rows, d = y.shape
    tm, tn = ROW_TILE, 512
    return pl.pallas_call(
        _outproj_kernel,
        grid=(rows // tm, d // tn),
        in_specs=[
            pl.BlockSpec((tm, d), lambda m, j: (m, 0)),
            pl.BlockSpec((d, tn), lambda m, j: (0, j)),
            pl.BlockSpec((tm, tn), lambda m, j: (m, j)),
            pl.BlockSpec((1, MOD_ROWS, tn), lambda m, j: (owner_of_tile(m), 0, j)),
        ],
        out_specs=pl.BlockSpec((tm, tn), lambda m, j: (m, j)),
        out_shape=jax.ShapeDtypeStruct((rows, d), F32),
        compiler_params=_params(("arbitrary", "arbitrary")),
        name="outproj",
    )(y, w_out, xf, mod_l)


FFN_OUT_SLAB = 512


def _ffn_kernel(*refs, final):
    if final:
        x_ref, mod_ref, g_ref, wg_ref, wu_ref, wd_ref, fg_ref, out_ref, h_ref = refs
    else:
        x_ref, mod_ref, g_ref, wg_ref, wu_ref, wd_ref, out_ref, h_ref = refs
    f = pl.program_id(1)

    @pl.when(f == 0)
    def _():
        for r0 in range(0, x_ref.shape[0], ROW_SLAB):
            rows = slice(r0, r0 + ROW_SLAB)
            h = _norm_mod(x_ref[rows, :], g_ref[...], mod_ref[0, 3:4, :], mod_ref[0, 4:5, :])
            h_ref[rows, :] = h.astype(BF16)
        out_ref[...] = jnp.zeros_like(out_ref)

    hb = h_ref[...]
    a = (_silu(_dot(hb, wg_ref[...])) * _dot(hb, wu_ref[...])).astype(BF16)
    d = out_ref.shape[1]
    for c0 in range(0, d, FFN_OUT_SLAB):
        out_ref[:, c0:c0 + FFN_OUT_SLAB] += _dot(a, wd_ref[:, c0:c0 + FFN_OUT_SLAB])

    @pl.when(f == pl.num_programs(1) - 1)
    def _():
        for r0 in range(0, x_ref.shape[0], ROW_SLAB):
            rows = slice(r0, r0 + ROW_SLAB)
            x = x_ref[rows, :] + mod_ref[0, 5:6, :] * out_ref[rows, :]
            if final:
                x = x * lax.rsqrt(jnp.mean(x * x, axis=-1, keepdims=True) + NORM_EPS) * fg_ref[...]
            out_ref[rows, :] = x


def _ffn_call(xf, mod_l, g, wg, wu, wd, owner_of_tile, out_rows, final_g=None):
    d = xf.shape[1]
    d_ff = wg.shape[1]
    tm, tf = ROW_TILE, 256
    final = final_g is not None
    in_specs = [
        pl.BlockSpec((tm, d), lambda m, f: (m, 0)),
        pl.BlockSpec((1, MOD_ROWS, d), lambda m, f: (owner_of_tile(m), 0, 0)),
        pl.BlockSpec((1, d), lambda m, f: (0, 0)),
        pl.BlockSpec((d, tf), lambda m, f: (0, f)),
        pl.BlockSpec((d, tf), lambda m, f: (0, f)),
        pl.BlockSpec((tf, d), lambda m, f: (f, 0)),
    ]
    args = [xf, mod_l, g, wg, wu, wd]
    if final:
        in_specs.append(pl.BlockSpec((1, d), lambda m, f: (0, 0)))
        args.append(final_g)
    return pl.pallas_call(
        functools.partial(_ffn_kernel, final=final),
        grid=(out_rows // tm, d_ff // tf),
        in_specs=in_specs,
        out_specs=pl.BlockSpec((tm, d), lambda m, f: (m, 0)),
        out_shape=jax.ShapeDtypeStruct((out_rows, d), F32),
        scratch_shapes=[pltpu.VMEM((tm, d), BF16)],
        compiler_params=_params(("arbitrary", "arbitrary")),
        name="ffn",
    )(*args)


def _rope_tables(t):
    rows = t // GRID_W
    half = HEAD_DIM // 4
    freq = ROPE_BASE ** (-jnp.arange(half, dtype=F32) / half)
    ang_r = jnp.arange(rows, dtype=F32)[:, None] * freq
    ang_c = jnp.arange(GRID_W, dtype=F32)[:, None] * freq
    ang = jnp.concatenate([jnp.broadcast_to(ang_r[:, None, :], (rows, GRID_W, half)),
                           jnp.broadcast_to(ang_c[None, :, :], (rows, GRID_W, half))], axis=-1)
    ang = ang.reshape(rows * GRID_W, 2 * half)
    return jnp.cos(ang), jnp.sin(ang)


def kernel(x, c, ctx, c_ctx, ada_w, ada_b, norm_mix_g, norm_ffn_g, w_in, m_gate_b, m_conv_w, m_norm_g, r_decay_logit, r_norm_g, s_w, s_b, w_up_m, w_up_r, w_up_s, w_out, ffn_w_gate, ffn_w_up, ffn_w_down, final_norm_g):
    batch, t, d_model = x.shape
    ctx_len = ctx.shape[1]
    depth = ada_w.shape[0]
    lat_rows = batch * t
    ctx_rows = batch * ctx_len
    assert t % ROW_TILE == 0 and ctx_rows <= ROW_TILE and batch + 1 <= MOD_ROWS
    assert t % PREP_ROWS == 0 and ctx_len % PREP_ROWS == 0 and t % GRID_W == 0
    geom = _ScanGeom(batch, t, ctx_len)
    seq_rows = geom.seq_rows
    tiles_per_batch = t // ROW_TILE
    all_rows = lat_rows + ROW_TILE

    def owner_of_tile(m):
        return jnp.minimum(m // tiles_per_batch, batch)

    xf = jnp.concatenate([x.reshape(lat_rows, d_model), ctx.reshape(ctx_rows, d_model),
                          jnp.zeros((ROW_TILE - ctx_rows, d_model), F32)], axis=0)

    cond = jnp.concatenate([c, c_ctx[None, :], jnp.zeros((MOD_ROWS - batch - 1, d_model), F32)], axis=0)
    mod = _ada_call(cond.T, ada_w, ada_b, batch + 1)
    mod = mod.reshape(depth, MOD_ROWS, 6, d_model)
    mod = jnp.pad(mod, ((0, 0), (0, 0), (0, MOD_ROWS - 6), (0, 0)))

    cos, sin = _rope_tables(t)

    g0 = 4 * WIDTH
    w_in_b = w_in.astype(BF16)
    w_main = jnp.concatenate([w_in_b[:, :, :g0], w_in_b[:, :, g0 + GATE_COLS:]], axis=-1)
    w_gate = jnp.pad(w_in_b[:, :, g0:g0 + GATE_COLS], ((0, 0), (0, 0), (0, GATE_PAD - GATE_COLS)))
    gate_bias = jnp.pad(m_gate_b.reshape(depth, 1, GATE_COLS), ((0, 0), (0, 0), (0, GATE_PAD - GATE_COLS)))
    conv_w = jnp.pad(m_conv_w, ((0, 0), (0, MOD_ROWS - CONV_TAPS), (0, 0)))
    decay_rows = jnp.pad(jnp.concatenate([r_decay_logit, r_decay_logit], axis=-1),
                         ((0, 0), (0, 0), (0, 128 - 2 * HEADS)))
    s_b_t = jnp.swapaxes(s_b, 1, 2)

    for l in range(depth):
        last = l == depth - 1
        z, gates = _inproj_call(xf, mod[l], norm_mix_g[l][None, :], w_main[l], w_gate[l], owner_of_tile)

        mq, mkt, rq, rkt = _prep_call(z, conv_w[l], cos, sin, seq_rows, lat_rows, t, ctx_len)
        gcol, grow = _gateprep_call(gates, gate_bias[l], seq_rows)
        hm = _mlstm_call(z, mq, mkt, gcol, grow, geom, 0)
        ym = _mlstm_call(z, mq, mkt, gcol, grow, geom, 1, hprev=hm, norm_g=m_norm_g[l][None, :], out_rows=all_rows)

        hr = _ret_call(z, rq, rkt, decay_rows[l, 0][None, :], geom, 0)
        yr = _ret_call(z, rq, rkt, decay_rows[l, 1][None, :], geom, 1, hprev=hr, norm_g=r_norm_g[l][None, :],
                       out_rows=all_rows)

        ys = _cmlp_call(z, s_w[l].astype(BF16), s_b_t[l])

        rows_out = lat_rows if last else all_rows
        y = _merge_call(ym, yr, ys, w_up_m[l].astype(BF16), w_up_r[l].astype(BF16), w_up_s[l].astype(BF16),
                        z, d_model, rows_out)
        xf = _outproj_call(y, w_out[l].astype(BF16), xf, mod[l], owner_of_tile)
        xf = _ffn_call(xf, mod[l], norm_ffn_g[l][None, :], ffn_w_gate[l].astype(BF16), ffn_w_up[l].astype(BF16),
                       ffn_w_down[l].astype(BF16), owner_of_tile, rows_out,
                       final_g=final_norm_g[None, :] if last else None)
    return xf.reshape(batch, t, d_model)
```

```python
import functools

import jax
import jax.numpy as jnp
from jax import lax
from jax.experimental import pallas as pl
from jax.experimental.pallas import tpu as pltpu

F32 = jnp.float32
BF16 = jnp.bfloat16

CHUNK = 128
GRID_W = 64
HEADS = 4
HEAD_DIM = 256
WIDTH = HEADS * HEAD_DIM
CONV_TAPS = 5
GATE_COLS = 16
GATE_PAD = 128
NORM_TILE = 128
GATE_ROWS = 32
S_GROUPS = 8
S_GROUP_DIM = 128
ROPE_BASE = 10000.0
NORM_EPS = 1e-6
HALO = 16
PREP_ROWS = 256
SCAN_CHUNKS = 2
SCAN_ROWS = SCAN_CHUNKS * CHUNK
MOD_ROWS = 8

ROW_TILE = 1024
ROW_SLAB = 256
VMEM_LIMIT = 56 * 1024 * 1024

COL_MQ, COL_MK, COL_MV, COL_MO, COL_RQ, COL_RK, COL_RV, COL_RG, COL_SU, COL_SV = range(10)
COL_GATES = 10


def _sigmoid(x):
    return 1.0 / (1.0 + jnp.exp(-x))


def _silu(x):
    return x / (1.0 + jnp.exp(-x))


def _log_sigmoid(x):
    return jnp.minimum(x, 0.0) - jnp.log(1.0 + jnp.exp(-jnp.abs(x)))


def _dot(a, b):
    return jnp.dot(a, b, preferred_element_type=F32)


def _params(sem, vmem=VMEM_LIMIT):
    return pltpu.CompilerParams(dimension_semantics=sem, vmem_limit_bytes=vmem)


def _ada_kernel(s_ref, w_ref, b_ref, o_ref, *, owners):
    s = _silu(s_ref[...])
    w = w_ref[0]
    rows = [jnp.sum(w * s[:, m:m + 1], axis=0, keepdims=True) for m in range(owners)]
    rows.append(jnp.zeros((MOD_ROWS - owners, w.shape[1]), F32))
    o_ref[0] = jnp.concatenate(rows, axis=0) + b_ref[0]


def _ada_call(cond_t, ada_w, ada_b, owners):
    depth, d, n = ada_w.shape
    tn = 512
    return pl.pallas_call(
        functools.partial(_ada_kernel, owners=owners),
        grid=(depth, n // tn),
        in_specs=[
            pl.BlockSpec((d, MOD_ROWS), lambda l, j: (0, 0)),
            pl.BlockSpec((1, d, tn), lambda l, j: (l, 0, j)),
            pl.BlockSpec((1, 1, tn), lambda l, j: (l, 0, j)),
        ],
        out_specs=pl.BlockSpec((1, MOD_ROWS, tn), lambda l, j: (l, 0, j)),
        out_shape=jax.ShapeDtypeStruct((depth, MOD_ROWS, n), F32),
        compiler_params=_params(("arbitrary", "arbitrary")),
        name="adaln",
    )(cond_t, ada_w, ada_b.reshape(depth, 1, n))


def _norm_mod(x, g, shift, scale):
    y = x * lax.rsqrt(jnp.mean(x * x, axis=-1, keepdims=True) + NORM_EPS) * g
    return y * (1.0 + scale) + shift


def _inproj_kernel(x_ref, mod_ref, g_ref, wa_ref, wb_ref, wg_ref, z_ref, gate_ref, hn_ref, *, tiles_a):
    j = pl.program_id(1)

    @pl.when(j == 0)
    def _():
        for r0 in range(0, x_ref.shape[0], ROW_SLAB):
            rows = slice(r0, r0 + ROW_SLAB)
            h = _norm_mod(x_ref[rows, :], g_ref[...], mod_ref[0, 0:1, :], mod_ref[0, 1:2, :])
            hb = h.astype(BF16)
            hn_ref[rows, :] = hb
            gate_ref[rows, :] = _dot(hb, wg_ref[...])

    @pl.when(j < tiles_a)
    def _():
        z_ref[...] = _dot(hn_ref[...], wa_ref[...]).astype(z_ref.dtype)

    @pl.when(j >= tiles_a)
    def _():
        z_ref[...] = _dot(hn_ref[...], wb_ref[...]).astype(z_ref.dtype)


def _inproj_call(xf, mod_l, g, w_a, w_b, w_gate, owner_of_tile):
    rows, d = xf.shape
    tm, tn = ROW_TILE, 1024
    tiles_a = w_a.shape[1] // tn
    n = w_a.shape[1] + w_b.shape[1]
    return pl.pallas_call(
        functools.partial(_inproj_kernel, tiles_a=tiles_a),
        grid=(rows // tm, n // tn),
        in_specs=[
            pl.BlockSpec((tm, d), lambda m, j: (m, 0)),
            pl.BlockSpec((1, MOD_ROWS, d), lambda m, j: (owner_of_tile(m), 0, 0)),
            pl.BlockSpec((1, d), lambda m, j: (0, 0)),
            pl.BlockSpec((d, tn), lambda m, j: (0, jnp.minimum(j, tiles_a - 1))),
            pl.BlockSpec((d, tn), lambda m, j: (0, jnp.maximum(j - tiles_a, 0))),
            pl.BlockSpec((d, GATE_PAD), lambda m, j: (0, 0)),
        ],
        out_specs=[
            pl.BlockSpec((tm, tn), lambda m, j: (m, j)),
            pl.BlockSpec((tm, GATE_PAD), lambda m, j: (m, 0)),
        ],
        out_shape=[
            jax.ShapeDtypeStruct((rows, n), BF16),
            jax.ShapeDtypeStruct((rows, GATE_PAD), F32),
        ],
        scratch_shapes=[pltpu.VMEM((tm, d), BF16)],
        compiler_params=_params(("arbitrary", "arbitrary")),
        name="inproj",
    )(xf, mod_l, g, w_a, w_b, w_gate)


def _transpose_chunks(k, kt_ref):
    for c in range(PREP_ROWS // CHUNK):
        for h in range(HEADS):
            blk = k[c * CHUNK:(c + 1) * CHUNK, h * HEAD_DIM:(h + 1) * HEAD_DIM]
            kt_ref[c, h * HEAD_DIM:(h + 1) * HEAD_DIM, :] = blk.T.astype(kt_ref.dtype)


def _prep_kernel(p_ref, cur_ref, nx_ref, cw_ref, r_ref, cos_ref, sin_ref,
                 mq_ref, mkt_ref, rq_ref, rkt_ref, ext_ref, *, lat_tiles, lat_per_seq, ctx_per_seq):
    i = pl.program_id(0)
    is_lat = i < lat_tiles
    pos = jnp.where(is_lat, i % lat_per_seq, (i - lat_tiles) % ctx_per_seq)
    per_seq = jnp.where(is_lat, lat_per_seq, ctx_per_seq)
    halo_zero = jnp.zeros(p_ref.shape, p_ref.dtype)
    ext_ref[0:HALO, :] = jnp.where(pos == 0, halo_zero, p_ref[...])
    ext_ref[HALO:HALO + PREP_ROWS, :] = cur_ref[...]
    ext_ref[HALO + PREP_ROWS:, :] = jnp.where(pos == per_seq - 1, halo_zero, nx_ref[...])

    window = CHUNK + 2 * HALO
    out_row = lax.broadcasted_iota(jnp.int32, (CHUNK, window), 0)
    in_row = lax.broadcasted_iota(jnp.int32, (CHUNK, window), 1)
    centre = CONV_TAPS // 2
    side_taps = [tap for tap in range(CONV_TAPS) if tap != centre]
    select = jnp.concatenate([(in_row == out_row + HALO + tap - centre).astype(BF16) for tap in side_taps], axis=0)
    for half in range(2):
        cols = slice(half * WIDTH, (half + 1) * WIDTH)
        blocks = []
        for r0 in range(0, PREP_ROWS, CHUNK):
            shifted = _dot(select, ext_ref[r0:r0 + window, cols])
            acc = ext_ref[r0 + HALO:r0 + HALO + CHUNK, cols].astype(F32) * cw_ref[centre:centre + 1, cols]
            for n, tap in enumerate(side_taps):
                acc = acc + shifted[n * CHUNK:(n + 1) * CHUNK, :] * cw_ref[tap:tap + 1, cols]
            blocks.append(_silu(acc))
        y = jnp.concatenate(blocks, axis=0)
        if half == 0:
            mq_ref[...] = y.astype(mq_ref.dtype)
        else:
            _transpose_chunks(y * (HEAD_DIM ** -0.5), mkt_ref)

    cos = jnp.where(is_lat, cos_ref[...], 1.0)
    sin = jnp.where(is_lat, sin_ref[...], 0.0)
    half_dim = HEAD_DIM // 2
    for part in range(2):
        pieces = []
        for h in range(HEADS):
            c0 = part * WIDTH + h * HEAD_DIM
            x1 = r_ref[:, c0:c0 + half_dim].astype(F32)
            x2 = r_ref[:, c0 + half_dim:c0 + HEAD_DIM].astype(F32)
            pieces += [x1 * cos - x2 * sin, x2 * cos + x1 * sin]
        rot = jnp.concatenate(pieces, axis=-1)
        if part == 0:
            rq_ref[...] = (rot * (HEAD_DIM ** -0.5)).astype(rq_ref.dtype)
        else:
            _transpose_chunks(rot, rkt_ref)


def _prep_call(z, conv_w, cos, sin, seq_rows, lat_rows, lat_len, ctx_len):
    tiles = seq_rows // PREP_ROWS
    per = PREP_ROWS // HALO
    blocks = seq_rows // HALO
    lat_per_seq = lat_len // PREP_ROWS
    ctx_per_seq = ctx_len // PREP_ROWS
    lat_tiles = lat_rows // PREP_ROWS
    cpt = PREP_ROWS // CHUNK

    def table():
        return pl.BlockSpec((PREP_ROWS, HEAD_DIM // 2), lambda i: (jnp.where(i < lat_tiles, i % lat_per_seq, 0), 0))

    def q_out():
        return pl.BlockSpec((PREP_ROWS, WIDTH), lambda i: (i, 0))

    def kt_out():
        return pl.BlockSpec((cpt, WIDTH, CHUNK), lambda i: (i, 0, 0))

    q_shape = jax.ShapeDtypeStruct((seq_rows, WIDTH), BF16)
    kt_shape = jax.ShapeDtypeStruct((seq_rows // CHUNK, WIDTH, CHUNK), BF16)
    return pl.pallas_call(
        functools.partial(_prep_kernel, lat_tiles=lat_tiles, lat_per_seq=lat_per_seq, ctx_per_seq=ctx_per_seq),
        grid=(tiles,),
        in_specs=[
            pl.BlockSpec((HALO, 2 * WIDTH), lambda i: (jnp.maximum(i * per - 1, 0), 0)),
            pl.BlockSpec((PREP_ROWS, 2 * WIDTH), lambda i: (i, 0)),
            pl.BlockSpec((HALO, 2 * WIDTH), lambda i: (jnp.minimum((i + 1) * per, blocks - 1), 0)),
            pl.BlockSpec((MOD_ROWS, 2 * WIDTH), lambda i: (0, 0)),
            pl.BlockSpec((PREP_ROWS, 2 * WIDTH), lambda i: (i, COL_RQ // 2)),
            table(), table(),
        ],
        out_specs=[q_out(), kt_out(), q_out(), kt_out()],
        out_shape=[q_shape, kt_shape, q_shape, kt_shape],
        scratch_shapes=[pltpu.VMEM((PREP_ROWS + 2 * HALO, 2 * WIDTH), BF16)],
        compiler_params=_params(("arbitrary",)),
        name="qkprep",
    )(z, z, z, conv_w, z, cos, sin)


def _gateprep_kernel(g_ref, b_ref, col_ref, row_ref):
    for c in range(col_ref.shape[0] // CHUNK):
        rows = slice(c * CHUNK, (c + 1) * CHUNK)
        _gateprep_chunk(g_ref[rows, :] + b_ref[...], col_ref.at[rows, :], row_ref.at[c])


def _gateprep_chunk(g, col_ref, row_ref):
    lane = lax.broadcasted_iota(jnp.int32, (CHUNK, GATE_PAD), 1)
    t = lax.broadcasted_iota(jnp.int32, (CHUNK, GATE_PAD), 0)
    is_forget = (lane % 8) >= HEADS
    is_backward = lane >= 8
    lf = _log_sigmoid(g)
    pre = lf
    suf = lf
    s = 1
    while s < CHUNK:
        pre = pre + jnp.where(t >= s, pltpu.roll(pre, s, 0), 0.0)
        suf = suf + jnp.where(t < CHUNK - s, pltpu.roll(suf, CHUNK - s, 0), 0.0)
        s *= 2
    col = jnp.where(is_forget, jnp.where(is_backward, suf, pre), g)

    cum_f = pltpu.roll(col, GATE_PAD - HEADS, 1)
    diff = col - cum_f
    pmax = diff
    smax = diff
    s = 1
    while s < CHUNK:
        pmax = jnp.maximum(pmax, jnp.where(t >= s, pltpu.roll(pmax, s, 0), -jnp.inf))
        smax = jnp.maximum(smax, jnp.where(t < CHUNK - s, pltpu.roll(smax, CHUNK - s, 0), -jnp.inf))
        s *= 2
    dmax = cum_f + jnp.where(is_backward, smax, pmax)
    col = jnp.where(lane < GATE_COLS, col, pltpu.roll(dmax, GATE_COLS, 1))
    col_ref[...] = col

    col_t = col.T
    row_id = lax.broadcasted_iota(jnp.int32, (CHUNK, CHUNK), 0)
    end_val = jnp.where((row_id % GATE_COLS) >= 8, col_t[:, 0:1], col_t[:, CHUNK - 1:CHUNK])
    row_ref[0:GATE_ROWS, :] = col_t[0:GATE_ROWS, :]
    row_ref[GATE_ROWS:, :] = jnp.broadcast_to(end_val, (CHUNK, CHUNK))[0:GATE_ROWS, :]


def _gateprep_call(gates, bias_row, seq_rows):
    nchunks = seq_rows // CHUNK
    cpt = PREP_ROWS // CHUNK
    return pl.pallas_call(
        _gateprep_kernel,
        grid=(nchunks // cpt,),
        in_specs=[
            pl.BlockSpec((PREP_ROWS, GATE_PAD), lambda i: (i, 0)),
            pl.BlockSpec((1, GATE_PAD), lambda i: (0, 0)),
        ],
        out_specs=[
            pl.BlockSpec((PREP_ROWS, GATE_PAD), lambda i: (i, 0)),
            pl.BlockSpec((cpt, 2 * GATE_ROWS, CHUNK), lambda i: (i, 0, 0)),
        ],
        out_shape=[
            jax.ShapeDtypeStruct((seq_rows, GATE_PAD), F32),
            jax.ShapeDtypeStruct((nchunks, 2 * GATE_ROWS, CHUNK), F32),
        ],
        compiler_params=_params(("arbitrary",)),
        name="gateprep",
    )(gates, bias_row)


class _ScanGeom:
    def __init__(self, batch, t, ctx_len):
        self.batch = batch
        self.nlat = t // SCAN_ROWS
        self.nctx = ctx_len // SCAN_ROWS
        self.steps = self.nlat + self.nctx
        self.seq_rows = batch * (t + ctx_len)

    def chunk_block(self, d, b, j):
        if d == 0:
            ctx_i, lat_i = j, j - self.nctx
        else:
            ctx_i, lat_i = self.nctx - 1 - j, self.steps - 1 - j
        return jnp.where(j < self.nctx,
                         self.batch * self.nlat + b * self.nctx + ctx_i,
                         b * self.nlat + lat_i)


def _headnorm(y, g):
    yc = y - jnp.mean(y, axis=-1, keepdims=True)
    return yc * lax.rsqrt(jnp.mean(yc * yc, axis=-1, keepdims=True) + NORM_EPS) * g


def _causal_mask(d):
    t = lax.broadcasted_iota(jnp.int32, (CHUNK, CHUNK), 0)
    s = lax.broadcasted_iota(jnp.int32, (CHUNK, CHUNK), 1)
    return (t >= s) if d == 0 else (s >= t)


def _mlstm_kernel(*refs, d, final):
    if final:
        (q_ref, kt_ref, v_ref, col_ref, row_ref, hprev_ref, mo_ref, ng_ref, _zero_init,
         out_ref, c_ref, m_ref) = refs
    else:
        q_ref, kt_ref, v_ref, col_ref, row_ref, out_ref, c_ref, m_ref = refs
        hprev_ref = mo_ref = ng_ref = None

    @pl.when(pl.program_id(1) == 0)
    def _():
        c_ref[...] = jnp.zeros_like(c_ref)
        m_ref[...] = jnp.full(m_ref.shape, -jnp.inf, F32)

    for c in _scan_order(d):
        rows = slice(c * CHUNK, (c + 1) * CHUNK)
        _mlstm_chunk(q_ref.at[rows, :], kt_ref.at[c], v_ref.at[rows, :], col_ref.at[rows, :], row_ref.at[c],
                     hprev_ref.at[rows, :] if final else None, mo_ref.at[rows, :] if final else None, ng_ref,
                     out_ref.at[rows, :], c_ref, m_ref, d, final)


def _scan_order(d):
    return range(SCAN_CHUNKS) if d == 0 else range(SCAN_CHUNKS - 1, -1, -1)


def _mlstm_chunk(q_ref, kt_ref, v_ref, col_ref, row_ref, hprev_ref, mo_ref, ng_ref, out_ref, c_ref, m_ref, d, final):
    col = col_ref[...]
    row = row_ref[...]
    mask = _causal_mask(d)
    heads = range(HEADS)
    sls = [slice(h * HEAD_DIM, (h + 1) * HEAD_DIM) for h in heads]

    ones_col = (lax.broadcasted_iota(jnp.int32, (CHUNK, NORM_TILE), 1) == 0).astype(BF16)
    v_ext = [jnp.concatenate([v_ref[:, sls[h]], ones_col], axis=1) for h in heads]

    c_old = [c_ref[h] for h in heads]
    qk = [_dot(q_ref[:, sls[h]], kt_ref[sls[h], :]) for h in heads]
    qc = [_dot(q_ref[:, sls[h]], c_old[h].astype(BF16)) for h in heads]

    i_idx = [d * 8 + h for h in heads]
    f_idx = [d * 8 + HEADS + h for h in heads]
    bc = [col[:, f_idx[h]:f_idx[h] + 1] for h in heads]
    dmax = [col[:, GATE_COLS + i_idx[h]:GATE_COLS + i_idx[h] + 1] for h in heads]
    li_r = [row[i_idx[h]:i_idx[h] + 1, :] for h in heads]
    br = [row[f_idx[h]:f_idx[h] + 1, :] for h in heads]
    b_end = [row[GATE_ROWS + f_idx[h]:GATE_ROWS + f_idx[h] + 1, :] for h in heads]
    a_max = [row[GATE_ROWS + GATE_COLS + i_idx[h]:GATE_ROWS + GATE_COLS + i_idx[h] + 1, :] for h in heads]
    m_prev = [m_ref[h:h + 1, :] for h in heads]
    inter = [bc[h] + m_prev[h][:, 0:1] for h in heads]
    m_t = [jnp.maximum(inter[h], dmax[h]) for h in heads]
    w_inter = [jnp.exp(inter[h] - m_t[h]) for h in heads]
    floor = [jnp.exp(-m_t[h]) for h in heads]
    decay_mat = [jnp.exp(jnp.where(mask, (bc[h] - m_t[h]) + (li_r[h] - br[h]), -jnp.inf)) for h in heads]
    m_new = [jnp.maximum(b_end[h] + m_prev[h], a_max[h]) for h in heads]
    decay = [jnp.exp(b_end[h] + m_prev[h] - m_new[h]) for h in heads]
    w_row = [jnp.exp(b_end[h] - br[h] + li_r[h] - m_new[h]) for h in heads]
    kw_t = [(kt_ref[sls[h], :].astype(F32) * w_row[h]).astype(BF16) for h in heads]

    s = [(qk[h] * decay_mat[h]).astype(BF16) for h in heads]
    sv = [_dot(s[h], v_ext[h]) for h in heads]
    kv = [_dot(kw_t[h], v_ext[h]) for h in heads]
    tot = [sv[h] + w_inter[h] * qc[h] for h in heads]
    scale = [1.0 / jnp.maximum(jnp.abs(tot[h][:, HEAD_DIM:HEAD_DIM + 1]), floor[h]) for h in heads]
    hout = [tot[h][:, 0:HEAD_DIM] * scale[h] for h in heads]

    for h in heads:
        decay_ext = jnp.concatenate([decay[h]] * (c_ref.shape[2] // CHUNK), axis=1)
        c_ref[h] = decay_ext * c_old[h] + kv[h]
        m_ref[h:h + 1, :] = m_new[h]
    for h in heads:
        if final:
            y = _sigmoid(mo_ref[:, sls[h]].astype(F32)) * (hout[h] + hprev_ref[:, sls[h]])
            out_ref[:, sls[h]] = _headnorm(y, ng_ref[:, sls[h]]).astype(out_ref.dtype)
        else:
            out_ref[:, sls[h]] = hout[h]


def _final_branch_io(in_specs, args, hprev, z_gate_spec, z, norm_g, out_rows):
    in_specs += [hprev[1], z_gate_spec, pl.BlockSpec((1, WIDTH), lambda b, j: (0, 0)),
                 pl.BlockSpec(memory_space=pl.ANY)]
    args += [hprev[0], z, norm_g, jnp.zeros((out_rows, WIDTH), BF16)]
    return {len(args) - 1: 0}


def _mlstm_call(z, q, kt, gcol, grow, geom, d, hprev=None, norm_g=None, out_rows=None):
    final = hprev is not None

    def cur(colblk):
        return pl.BlockSpec((SCAN_ROWS, WIDTH), lambda b, j: (geom.chunk_block(d, b, j), colblk))

    in_specs = [
        cur(0),
        pl.BlockSpec((SCAN_CHUNKS, WIDTH, CHUNK), lambda b, j: (geom.chunk_block(d, b, j), 0, 0)),
        cur(COL_MV),
        pl.BlockSpec((SCAN_ROWS, GATE_PAD), lambda b, j: (geom.chunk_block(d, b, j), 0)),
        pl.BlockSpec((SCAN_CHUNKS, 2 * GATE_ROWS, CHUNK), lambda b, j: (geom.chunk_block(d, b, j), 0, 0)),
    ]
    args = [q, kt, z, gcol, grow]
    aliases = {}
    if final:
        aliases = _final_branch_io(in_specs, args, (hprev, cur(0)), cur(COL_MO), z, norm_g, out_rows)
    return pl.pallas_call(
        functools.partial(_mlstm_kernel, d=d, final=final),
        grid=(geom.batch, geom.steps),
        in_specs=in_specs,
        out_specs=cur(0),
        input_output_aliases=aliases,
        out_shape=jax.ShapeDtypeStruct((out_rows if final else geom.seq_rows, WIDTH), BF16 if final else F32),
        scratch_shapes=[
            pltpu.VMEM((HEADS, HEAD_DIM, HEAD_DIM + NORM_TILE), F32),
            pltpu.VMEM((8, CHUNK), F32),
        ],
        compiler_params=_params(("arbitrary", "arbitrary")),
        name="mlstm_bwd" if d else "mlstm_fwd",
    )(*args)


def _ret_kernel(*refs, d, final):
    if final:
        (q_ref, kt_ref, v_ref, lg_ref, hprev_ref, rg_ref, ng_ref, _zero_init,
         out_ref, r_ref, dm_ref, wx_ref, we_ref, dec_ref) = refs
    else:
        q_ref, kt_ref, v_ref, lg_ref, out_ref, r_ref, dm_ref, wx_ref, we_ref, dec_ref = refs

    @pl.when(pl.program_id(1) == 0)
    def _():
        r_ref[...] = jnp.zeros_like(r_ref)
        log_g = _log_sigmoid(lg_ref[...])
        t = lax.broadcasted_iota(jnp.int32, (CHUNK, CHUNK), 0)
        s = lax.broadcasted_iota(jnp.int32, (CHUNK, CHUNK), 1)
        rel = (t - s) if d == 0 else (s - t)
        relf = jnp.maximum(rel, 0).astype(F32)
        tf = t.astype(F32)
        sf = s[0:1, :].astype(F32)
        from_start = (tf + 1.0) if d == 0 else (CHUNK - tf)
        to_end = (CHUNK - 1.0 - sf) if d == 0 else sf
        for h in range(HEADS):
            lg = log_g[:, h:h + 1]
            dm_ref[h] = jnp.where(rel >= 0, jnp.exp(lg * relf), 0.0)
            wx_ref[h] = jnp.exp(lg * from_start)
            we_ref[h:h + 1, :] = jnp.exp(lg * to_end)
            dec_ref[h:h + 1, :] = jnp.broadcast_to(jnp.exp(lg * float(CHUNK)), (1, CHUNK))

    consts = (r_ref, dm_ref, wx_ref, we_ref, dec_ref)
    for c in _scan_order(d):
        rows = slice(c * CHUNK, (c + 1) * CHUNK)
        _ret_chunk(q_ref.at[rows, :], kt_ref.at[c], v_ref.at[rows, :],
                   hprev_ref.at[rows, :] if final else None, rg_ref.at[rows, :] if final else None,
                   ng_ref if final else None, out_ref.at[rows, :], consts, final)


def _ret_chunk(q_ref, kt_ref, v_ref, hprev_ref, rg_ref, ng_ref, out_ref, consts, final):
    r_ref, dm_ref, wx_ref, we_ref, dec_ref = consts
    heads = range(HEADS)
    sls = [slice(h * HEAD_DIM, (h + 1) * HEAD_DIM) for h in heads]
    reps = HEAD_DIM // CHUNK

    r_old = [r_ref[h] for h in heads]
    qk = [_dot(q_ref[:, sls[h]], kt_ref[sls[h], :]) for h in heads]
    qr = [_dot(q_ref[:, sls[h]], r_old[h].astype(BF16)) for h in heads]
    kw_t = [(kt_ref[sls[h], :].astype(F32) * we_ref[h:h + 1, :]).astype(BF16) for h in heads]
    s = [(qk[h] * dm_ref[h]).astype(BF16) for h in heads]
    sv = [_dot(s[h], v_ref[:, sls[h]]) for h in heads]
    kv = [_dot(kw_t[h], v_ref[:, sls[h]]) for h in heads]
    hout = [sv[h] + qr[h] * jnp.concatenate([wx_ref[h]] * reps, axis=1) for h in heads]
    for h in heads:
        r_ref[h] = jnp.concatenate([dec_ref[h:h + 1, :]] * reps, axis=1) * r_old[h] + kv[h]
    for h in heads:
        if final:
            y = _headnorm(hout[h] + hprev_ref[:, sls[h]], ng_ref[:, sls[h]]) * _silu(rg_ref[:, sls[h]].astype(F32))
            out_ref[:, sls[h]] = y.astype(out_ref.dtype)
        else:
            out_ref[:, sls[h]] = hout[h]


def _ret_call(z, q, kt, decay_logit_row, geom, d, hprev=None, norm_g=None, out_rows=None):
    final = hprev is not None

    def cur(colblk):
        return pl.BlockSpec((SCAN_ROWS, WIDTH), lambda b, j: (geom.chunk_block(d, b, j), colblk))

    in_specs = [cur(0),
                pl.BlockSpec((SCAN_CHUNKS, WIDTH, CHUNK), lambda b, j: (geom.chunk_block(d, b, j), 0, 0)),
                cur(COL_RV),
                pl.BlockSpec((1, 128), lambda b, j: (0, 0))]
    args = [q, kt, z, decay_logit_row]
    aliases = {}
    if final:
        aliases = _final_branch_io(in_specs, args, (hprev, cur(0)), cur(COL_RG), z, norm_g, out_rows)
    return pl.pallas_call(
        functools.partial(_ret_kernel, d=d, final=final),
        grid=(geom.batch, geom.steps),
        in_specs=in_specs,
        out_specs=cur(0),
        input_output_aliases=aliases,
        out_shape=jax.ShapeDtypeStruct((out_rows if final else geom.seq_rows, WIDTH), BF16 if final else F32),
        scratch_shapes=[
            pltpu.VMEM((HEADS, HEAD_DIM, HEAD_DIM), F32),
            pltpu.VMEM((HEADS, CHUNK, CHUNK), F32),
            pltpu.VMEM((HEADS, CHUNK, CHUNK), F32),
            pltpu.VMEM((8, CHUNK), F32),
            pltpu.VMEM((8, CHUNK), F32),
        ],
        compiler_params=_params(("arbitrary", "arbitrary")),
        name="ret_bwd" if d else "ret_fwd",
    )(*args)


def _cmlp_kernel(u_ref, v_ref, w_ref, b_ref, out_ref):
    parts = [(slice(c * CHUNK, (c + 1) * CHUNK), slice(g * S_GROUP_DIM, (g + 1) * S_GROUP_DIM), g)
             for c in range(out_ref.shape[0] // CHUNK) for g in range(S_GROUPS)]
    v = [v_ref[r, l].astype(F32) for r, l, _ in parts]
    vc = [x - jnp.mean(x, axis=-1, keepdims=True) for x in v]
    var = [jnp.mean(x * x, axis=-1, keepdims=True) for x in vc]
    vn = [(x * lax.rsqrt(s + NORM_EPS)).astype(BF16) for x, s in zip(vc, var)]
    mixed = [_dot(w_ref[g], x) + b_ref[:, g:g + 1] for x, (_, _, g) in zip(vn, parts)]
    for x, (r, l, _) in zip(mixed, parts):
        out_ref[r, l] = (u_ref[r, l].astype(F32) * x).astype(out_ref.dtype)


def _cmlp_call(z, s_w, s_b_t):
    seq_rows = z.shape[0]
    return pl.pallas_call(
        _cmlp_kernel,
        grid=(seq_rows // PREP_ROWS,),
        in_specs=[
            pl.BlockSpec((PREP_ROWS, WIDTH), lambda i: (i, COL_SU)),
            pl.BlockSpec((PREP_ROWS, WIDTH), lambda i: (i, COL_SV)),
            pl.BlockSpec((S_GROUPS, CHUNK, CHUNK), lambda i: (0, 0, 0)),
            pl.BlockSpec((CHUNK, S_GROUPS), lambda i: (0, 0)),
        ],
        out_specs=pl.BlockSpec((PREP_ROWS, WIDTH), lambda i: (i, 0)),
        out_shape=jax.ShapeDtypeStruct((seq_rows, WIDTH), BF16),
        compiler_params=_params(("arbitrary",)),
        name="chunk_mlp",
    )(z, z, s_w, s_b_t)


def _merge_kernel(ym_ref, yr_ref, ys_ref, wm_ref, wr_ref, ws_ref, gm_ref, gr_ref, gs_ref, out_ref):
    y = (_sigmoid(gm_ref[...].astype(F32)) * _dot(ym_ref[...], wm_ref[...])
         + _sigmoid(gr_ref[...].astype(F32)) * _dot(yr_ref[...], wr_ref[...])
         + _sigmoid(gs_ref[...].astype(F32)) * _dot(ys_ref[...], ws_ref[...]))
    out_ref[...] = y.astype(out_ref.dtype)


def _merge_call(ym, yr, ys, wm, wr, ws, z, d_model, out_rows):
    tm, tn = ROW_TILE, 512
    gate0 = COL_GATES * WIDTH // tn
    per = d_model // tn

    def ybranch():
        return pl.BlockSpec((tm, WIDTH), lambda m, j: (m, 0))

    def wbranch():
        return pl.BlockSpec((WIDTH, tn), lambda m, j: (0, j))

    def gate(i):
        return pl.BlockSpec((tm, tn), lambda m, j: (m, gate0 + i * per + j))

    return pl.pallas_call(
        _merge_kernel,
        grid=(out_rows // tm, per),
        in_specs=[ybranch(), ybranch(), ybranch(), wbranch(), wbranch(), wbranch(), gate(0), gate(1), gate(2)],
        out_specs=pl.BlockSpec((tm, tn), lambda m, j: (m, j)),
        out_shape=jax.ShapeDtypeStruct((out_rows, d_model), BF16),
        compiler_params=_params(("arbitrary", "arbitrary")),
        name="merge",
    )(ym, yr, ys, wm, wr, ws, z, z, z)


def _outproj_kernel(y_ref, w_ref, x_ref, mod_ref, out_ref):
    out_ref[...] = x_ref[...] + mod_ref[0, 2:3, :] * _dot(y_ref[...], w_ref[...])


def _outproj_call(y, w_out, xf, mod_l, owner_of_tile):
    rows, d = y.shape
    tm, tn = ROW_TILE, 512
    return pl.pallas_call(
        _outproj_kernel,
        grid=(rows // tm, d // tn),
        in_specs=[
            pl.BlockSpec((tm, d), lambda m, j: (m, 0)),
            pl.BlockSpec((d, tn), lambda m, j: (0, j)),
            pl.BlockSpec((tm, tn), lambda m, j: (m, j)),
            pl.BlockSpec((1, MOD_ROWS, tn), lambda m, j: (owner_of_tile(m), 0, j)),
        ],
        out_specs=pl.BlockSpec((tm, tn), lambda m, j: (m, j)),
        out_shape=jax.ShapeDtypeStruct((rows, d), F32),
        compiler_params=_params(("arbitrary", "arbitrary")),
        name="outproj",
    )(y, w_out, xf, mod_l)


FFN_OUT_SLAB = 512


def _ffn_kernel(*refs, final):
    if final:
        x_ref, mod_ref, g_ref, wg_ref, wu_ref, wd_ref, fg_ref, out_ref, h_ref = refs
    else:
        x_ref, mod_ref, g_ref, wg_ref, wu_ref, wd_ref, out_ref, h_ref = refs
    f = pl.program_id(1)

    @pl.when(f == 0)
    def _():
        for r0 in range(0, x_ref.shape[0], ROW_SLAB):
            rows = slice(r0, r0 + ROW_SLAB)
            h = _norm_mod(x_ref[rows, :], g_ref[...], mod_ref[0, 3:4, :], mod_ref[0, 4:5, :])
            h_ref[rows, :] = h.astype(BF16)
        out_ref[...] = jnp.zeros_like(out_ref)

    hb = h_ref[...]
    a = (_silu(_dot(hb, wg_ref[...])) * _dot(hb, wu_ref[...])).astype(BF16)
    d = out_ref.shape[1]
    for c0 in range(0, d, FFN_OUT_SLAB):
        out_ref[:, c0:c0 + FFN_OUT_SLAB] += _dot(a, wd_ref[:, c0:c0 + FFN_OUT_SLAB])

    @pl.when(f == pl.num_programs(1) - 1)
    def _():
        for r0 in range(0, x_ref.shape[0], ROW_SLAB):
            rows = slice(r0, r0 + ROW_SLAB)
            x = x_ref[rows, :] + mod_ref[0, 5:6, :] * out_ref[rows, :]
            if final:
                x = x * lax.rsqrt(jnp.mean(x * x, axis=-1, keepdims=True) + NORM_EPS) * fg_ref[...]
            out_ref[rows, :] = x


def _ffn_call(xf, mod_l, g, wg, wu, wd, owner_of_tile, out_rows, final_g=None):
    d = xf.shape[1]
    d_ff = wg.shape[1]
    tm, tf = ROW_TILE, 512
    final = final_g is not None
    in_specs = [
        pl.BlockSpec((tm, d), lambda m, f: (m, 0), pipeline_mode=pl.Buffered(1)),
        pl.BlockSpec((1, MOD_ROWS, d), lambda m, f: (owner_of_tile(m), 0, 0)),
        pl.BlockSpec((1, d), lambda m, f: (0, 0)),
        pl.BlockSpec((d, tf), lambda m, f: (0, f)),
        pl.BlockSpec((d, tf), lambda m, f: (0, f)),
        pl.BlockSpec((tf, d), lambda m, f: (f, 0)),
    ]
    args = [xf, mod_l, g, wg, wu, wd]
    if final:
        in_specs.append(pl.BlockSpec((1, d), lambda m, f: (0, 0)))
        args.append(final_g)
    return pl.pallas_call(
        functools.partial(_ffn_kernel, final=final),
        grid=(out_rows // tm, d_ff // tf),
        in_specs=in_specs,
        out_specs=pl.BlockSpec((tm, d), lambda m, f: (m, 0)),
        out_shape=jax.ShapeDtypeStruct((out_rows, d), F32),
        scratch_shapes=[pltpu.VMEM((tm, d), BF16)],
        compiler_params=_params(("arbitrary", "arbitrary")),
        name="ffn",
    )(*args)


def _rope_tables(t):
    rows = t // GRID_W
    half = HEAD_DIM // 4
    freq = ROPE_BASE ** (-jnp.arange(half, dtype=F32) / half)
    ang_r = jnp.arange(rows, dtype=F32)[:, None] * freq
    ang_c = jnp.arange(GRID_W, dtype=F32)[:, None] * freq
    ang = jnp.concatenate([jnp.broadcast_to(ang_r[:, None, :], (rows, GRID_W, half)),
                           jnp.broadcast_to(ang_c[None, :, :], (rows, GRID_W, half))], axis=-1)
    ang = ang.reshape(rows * GRID_W, 2 * half)
    return jnp.cos(ang), jnp.sin(ang)


def kernel(x, c, ctx, c_ctx, ada_w, ada_b, norm_mix_g, norm_ffn_g, w_in, m_gate_b, m_conv_w, m_norm_g, r_decay_logit, r_norm_g, s_w, s_b, w_up_m, w_up_r, w_up_s, w_out, ffn_w_gate, ffn_w_up, ffn_w_down, final_norm_g):
    batch, t, d_model = x.shape
    ctx_len = ctx.shape[1]
    depth = ada_w.shape[0]
    lat_rows = batch * t
    ctx_rows = batch * ctx_len
    assert t % ROW_TILE == 0 and ctx_rows <= ROW_TILE and batch + 1 <= MOD_ROWS
    assert t % PREP_ROWS == 0 and ctx_len % PREP_ROWS == 0 and t % GRID_W == 0
    assert t % SCAN_ROWS == 0 and ctx_len % SCAN_ROWS == 0
    geom = _ScanGeom(batch, t, ctx_len)
    seq_rows = geom.seq_rows
    tiles_per_batch = t // ROW_TILE
    all_rows = lat_rows + ROW_TILE

    def owner_of_tile(m):
        return jnp.minimum(m // tiles_per_batch, batch)

    xf = jnp.concatenate([x.reshape(lat_rows, d_model), ctx.reshape(ctx_rows, d_model),
                          jnp.zeros((ROW_TILE - ctx_rows, d_model), F32)], axis=0)

    cond = jnp.concatenate([c, c_ctx[None, :], jnp.zeros((MOD_ROWS - batch - 1, d_model), F32)], axis=0)
    mod = _ada_call(cond.T, ada_w, ada_b, batch + 1)
    mod = mod.reshape(depth, MOD_ROWS, 6, d_model)
    mod = jnp.pad(mod, ((0, 0), (0, 0), (0, MOD_ROWS - 6), (0, 0)))

    cos, sin = _rope_tables(t)

    g0 = 4 * WIDTH
    w_a = w_in[:, :, :g0].astype(BF16)
    w_b = w_in[:, :, g0 + GATE_COLS:].astype(BF16)
    w_gate = jnp.pad(w_in[:, :, g0:g0 + GATE_COLS], ((0, 0), (0, 0), (0, GATE_PAD - GATE_COLS))).astype(BF16)
    gate_bias = jnp.pad(m_gate_b.reshape(depth, 1, GATE_COLS), ((0, 0), (0, 0), (0, GATE_PAD - GATE_COLS)))
    conv_w = jnp.pad(m_conv_w, ((0, 0), (0, MOD_ROWS - CONV_TAPS), (0, 0)))
    decay_rows = jnp.pad(jnp.concatenate([r_decay_logit, r_decay_logit], axis=-1),
                         ((0, 0), (0, 0), (0, 128 - 2 * HEADS)))
    s_b_t = jnp.swapaxes(s_b, 1, 2)

    for l in range(depth):
        last = l == depth - 1
        z, gates = _inproj_call(xf, mod[l], norm_mix_g[l][None, :], w_a[l], w_b[l], w_gate[l], owner_of_tile)

        mq, mkt, rq, rkt = _prep_call(z, conv_w[l], cos, sin, seq_rows, lat_rows, t, ctx_len)
        gcol, grow = _gateprep_call(gates, gate_bias[l], seq_rows)
        hm = _mlstm_call(z, mq, mkt, gcol, grow, geom, 0)
        ym = _mlstm_call(z, mq, mkt, gcol, grow, geom, 1, hprev=hm, norm_g=m_norm_g[l][None, :], out_rows=all_rows)

        hr = _ret_call(z, rq, rkt, decay_rows[l, 0][None, :], geom, 0)
        yr = _ret_call(z, rq, rkt, decay_rows[l, 1][None, :], geom, 1, hprev=hr, norm_g=r_norm_g[l][None, :],
                       out_rows=all_rows)

        ys = _cmlp_call(z, s_w[l].astype(BF16), s_b_t[l])

        rows_out = lat_rows if last else all_rows
        y = _merge_call(ym, yr, ys, w_up_m[l].astype(BF16), w_up_r[l].astype(BF16), w_up_s[l].astype(BF16),
                        z, d_model, rows_out)
        xf = _outproj_call(y, w_out[l].astype(BF16), xf, mod[l], owner_of_tile)
        xf = _ffn_call(xf, mod[l], norm_ffn_g[l][None, :], ffn_w_gate[l].astype(BF16), ffn_w_up[l].astype(BF16),
                       ffn_w_down[l].astype(BF16), owner_of_tile, rows_out,
                       final_g=final_norm_g[None, :] if last else None)
    return xf.reshape(batch, t, d_model)
```

```python
import functools

import jax
import jax.numpy as jnp
from jax import lax
from jax.experimental import pallas as pl
from jax.experimental.pallas import tpu as pltpu

F32 = jnp.float32
BF16 = jnp.bfloat16

CHUNK = 128
GRID_W = 64
HEADS = 4
HEAD_DIM = 256
WIDTH = HEADS * HEAD_DIM
CONV_TAPS = 5
GATE_COLS = 16
GATE_PAD = 128
NORM_TILE = 128
GATE_ROWS = 32
S_GROUPS = 8
S_GROUP_DIM = 128
ROPE_BASE = 10000.0
NORM_EPS = 1e-6
HALO = 16
PREP_ROWS = 256
SCAN_CHUNKS = 2
SCAN_ROWS = SCAN_CHUNKS * CHUNK
MOD_ROWS = 8

ROW_TILE = 1024
ROW_SLAB = 256
VMEM_LIMIT = 56 * 1024 * 1024

COL_MQ, COL_MK, COL_MV, COL_MO, COL_RQ, COL_RK, COL_RV, COL_RG, COL_SU, COL_SV = range(10)
COL_GATES = 10


def _sigmoid(x):
    return 0.5 * jnp.tanh(0.5 * x) + 0.5


def _silu(x):
    return x * _sigmoid(x)


def _log_sigmoid(x):
    return jnp.minimum(x, 0.0) - jnp.log(1.0 + jnp.exp(-jnp.abs(x)))


def _dot(a, b):
    return jnp.dot(a, b, preferred_element_type=F32)


def _params(sem, vmem=VMEM_LIMIT):
    return pltpu.CompilerParams(dimension_semantics=sem, vmem_limit_bytes=vmem)


def _ada_kernel(s_ref, w_ref, b_ref, o_ref, *, owners):
    s = _silu(s_ref[...])
    w = w_ref[0]
    rows = [jnp.sum(w * s[:, m:m + 1], axis=0, keepdims=True) for m in range(owners)]
    rows.append(jnp.zeros((MOD_ROWS - owners, w.shape[1]), F32))
    o_ref[0] = jnp.concatenate(rows, axis=0) + b_ref[0]


def _ada_call(cond_t, ada_w, ada_b, owners):
    depth, d, n = ada_w.shape
    tn = 512
    return pl.pallas_call(
        functools.partial(_ada_kernel, owners=owners),
        grid=(depth, n // tn),
        in_specs=[
            pl.BlockSpec((d, MOD_ROWS), lambda l, j: (0, 0)),
            pl.BlockSpec((1, d, tn), lambda l, j: (l, 0, j)),
            pl.BlockSpec((1, 1, tn), lambda l, j: (l, 0, j)),
        ],
        out_specs=pl.BlockSpec((1, MOD_ROWS, tn), lambda l, j: (l, 0, j)),
        out_shape=jax.ShapeDtypeStruct((depth, MOD_ROWS, n), F32),
        compiler_params=_params(("arbitrary", "arbitrary")),
        name="adaln",
    )(cond_t, ada_w, ada_b.reshape(depth, 1, n))


def _norm_mod(x, g, shift, scale):
    y = x * lax.rsqrt(jnp.mean(x * x, axis=-1, keepdims=True) + NORM_EPS) * g
    return y * (1.0 + scale) + shift


def _inproj_kernel(x_ref, mod_ref, g_ref, wa_ref, wb_ref, wg_ref, z_ref, gate_ref, hn_ref, *, tiles_a):
    j = pl.program_id(1)

    @pl.when(j == 0)
    def _():
        for r0 in range(0, x_ref.shape[0], ROW_SLAB):
            rows = slice(r0, r0 + ROW_SLAB)
            h = _norm_mod(x_ref[rows, :], g_ref[...], mod_ref[0, 0:1, :], mod_ref[0, 1:2, :])
            hb = h.astype(BF16)
            hn_ref[rows, :] = hb
            gate_ref[rows, :] = _dot(hb, wg_ref[...])

    @pl.when(j < tiles_a)
    def _():
        z_ref[...] = _dot(hn_ref[...], wa_ref[...]).astype(z_ref.dtype)

    @pl.when(j >= tiles_a)
    def _():
        z_ref[...] = _dot(hn_ref[...], wb_ref[...]).astype(z_ref.dtype)


def _inproj_call(xf, mod_l, g, w_a, w_b, w_gate, owner_of_tile):
    rows, d = xf.shape
    tm, tn = ROW_TILE, 1024
    tiles_a = w_a.shape[1] // tn
    n = w_a.shape[1] + w_b.shape[1]
    return pl.pallas_call(
        functools.partial(_inproj_kernel, tiles_a=tiles_a),
        grid=(rows // tm, n // tn),
        in_specs=[
            pl.BlockSpec((tm, d), lambda m, j: (m, 0)),
            pl.BlockSpec((1, MOD_ROWS, d), lambda m, j: (owner_of_tile(m), 0, 0)),
            pl.BlockSpec((1, d), lambda m, j: (0, 0)),
            pl.BlockSpec((d, tn), lambda m, j: (0, jnp.minimum(j, tiles_a - 1))),
            pl.BlockSpec((d, tn), lambda m, j: (0, jnp.maximum(j - tiles_a, 0))),
            pl.BlockSpec((d, GATE_PAD), lambda m, j: (0, 0)),
        ],
        out_specs=[
            pl.BlockSpec((tm, tn), lambda m, j: (m, j)),
            pl.BlockSpec((tm, GATE_PAD), lambda m, j: (m, 0)),
        ],
        out_shape=[
            jax.ShapeDtypeStruct((rows, n), BF16),
            jax.ShapeDtypeStruct((rows, GATE_PAD), F32),
        ],
        scratch_shapes=[pltpu.VMEM((tm, d), BF16)],
        compiler_params=_params(("arbitrary", "arbitrary")),
        name="inproj",
    )(xf, mod_l, g, w_a, w_b, w_gate)


def _transpose_chunks(k, kt_ref):
    for c in range(PREP_ROWS // CHUNK):
        for h in range(HEADS):
            blk = k[c * CHUNK:(c + 1) * CHUNK, h * HEAD_DIM:(h + 1) * HEAD_DIM]
            kt_ref[c, h * HEAD_DIM:(h + 1) * HEAD_DIM, :] = blk.T.astype(kt_ref.dtype)


def _prep_kernel(p_ref, cur_ref, nx_ref, cw_ref, r_ref, cos_ref, sin_ref,
                 mq_ref, mkt_ref, rq_ref, rkt_ref, ext_ref, *, lat_tiles, lat_per_seq, ctx_per_seq):
    i = pl.program_id(0)
    is_lat = i < lat_tiles
    pos = jnp.where(is_lat, i % lat_per_seq, (i - lat_tiles) % ctx_per_seq)
    per_seq = jnp.where(is_lat, lat_per_seq, ctx_per_seq)
    halo_zero = jnp.zeros(p_ref.shape, p_ref.dtype)
    ext_ref[0:HALO, :] = jnp.where(pos == 0, halo_zero, p_ref[...])
    ext_ref[HALO:HALO + PREP_ROWS, :] = cur_ref[...]
    ext_ref[HALO + PREP_ROWS:, :] = jnp.where(pos == per_seq - 1, halo_zero, nx_ref[...])

    window = CHUNK + 2 * HALO
    out_row = lax.broadcasted_iota(jnp.int32, (CHUNK, window), 0)
    in_row = lax.broadcasted_iota(jnp.int32, (CHUNK, window), 1)
    centre = CONV_TAPS // 2
    side_taps = [tap for tap in range(CONV_TAPS) if tap != centre]
    select = jnp.concatenate([(in_row == out_row + HALO + tap - centre).astype(BF16) for tap in side_taps], axis=0)
    for half in range(2):
        cols = slice(half * WIDTH, (half + 1) * WIDTH)
        blocks = []
        for r0 in range(0, PREP_ROWS, CHUNK):
            shifted = _dot(select, ext_ref[r0:r0 + window, cols])
            acc = ext_ref[r0 + HALO:r0 + HALO + CHUNK, cols].astype(F32) * cw_ref[centre:centre + 1, cols]
            for n, tap in enumerate(side_taps):
                acc = acc + shifted[n * CHUNK:(n + 1) * CHUNK, :] * cw_ref[tap:tap + 1, cols]
            blocks.append(_silu(acc))
        y = jnp.concatenate(blocks, axis=0)
        if half == 0:
            mq_ref[...] = y.astype(mq_ref.dtype)
        else:
            _transpose_chunks(y * (HEAD_DIM ** -0.5), mkt_ref)

    cos = jnp.where(is_lat, cos_ref[...], 1.0)
    sin = jnp.where(is_lat, sin_ref[...], 0.0)
    half_dim = HEAD_DIM // 2
    for part in range(2):
        pieces = []
        for h in range(HEADS):
            c0 = part * WIDTH + h * HEAD_DIM
            x1 = r_ref[:, c0:c0 + half_dim].astype(F32)
            x2 = r_ref[:, c0 + half_dim:c0 + HEAD_DIM].astype(F32)
            pieces += [x1 * cos - x2 * sin, x2 * cos + x1 * sin]
        rot = jnp.concatenate(pieces, axis=-1)
        if part == 0:
            rq_ref[...] = (rot * (HEAD_DIM ** -0.5)).astype(rq_ref.dtype)
        else:
            _transpose_chunks(rot, rkt_ref)


def _prep_call(z, conv_w, cos, sin, seq_rows, lat_rows, lat_len, ctx_len):
    tiles = seq_rows // PREP_ROWS
    per = PREP_ROWS // HALO
    blocks = seq_rows // HALO
    lat_per_seq = lat_len // PREP_ROWS
    ctx_per_seq = ctx_len // PREP_ROWS
    lat_tiles = lat_rows // PREP_ROWS
    cpt = PREP_ROWS // CHUNK

    def table():
        return pl.BlockSpec((PREP_ROWS, HEAD_DIM // 2), lambda i: (jnp.where(i < lat_tiles, i % lat_per_seq, 0), 0))

    def q_out():
        return pl.BlockSpec((PREP_ROWS, WIDTH), lambda i: (i, 0))

    def kt_out():
        return pl.BlockSpec((cpt, WIDTH, CHUNK), lambda i: (i, 0, 0))

    q_shape = jax.ShapeDtypeStruct((seq_rows, WIDTH), BF16)
    kt_shape = jax.ShapeDtypeStruct((seq_rows // CHUNK, WIDTH, CHUNK), BF16)
    return pl.pallas_call(
        functools.partial(_prep_kernel, lat_tiles=lat_tiles, lat_per_seq=lat_per_seq, ctx_per_seq=ctx_per_seq),
        grid=(tiles,),
        in_specs=[
            pl.BlockSpec((HALO, 2 * WIDTH), lambda i: (jnp.maximum(i * per - 1, 0), 0)),
            pl.BlockSpec((PREP_ROWS, 2 * WIDTH), lambda i: (i, 0)),
            pl.BlockSpec((HALO, 2 * WIDTH), lambda i: (jnp.minimum((i + 1) * per, blocks - 1), 0)),
            pl.BlockSpec((MOD_ROWS, 2 * WIDTH), lambda i: (0, 0)),
            pl.BlockSpec((PREP_ROWS, 2 * WIDTH), lambda i: (i, COL_RQ // 2)),
            table(), table(),
        ],
        out_specs=[q_out(), kt_out(), q_out(), kt_out()],
        out_shape=[q_shape, kt_shape, q_shape, kt_shape],
        scratch_shapes=[pltpu.VMEM((PREP_ROWS + 2 * HALO, 2 * WIDTH), BF16)],
        compiler_params=_params(("arbitrary",)),
        name="qkprep",
    )(z, z, z, conv_w, z, cos, sin)


def _gateprep_kernel(g_ref, b_ref, col_ref, row_ref):
    for c in range(col_ref.shape[0] // CHUNK):
        rows = slice(c * CHUNK, (c + 1) * CHUNK)
        _gateprep_chunk(g_ref[rows, :] + b_ref[...], col_ref.at[rows, :], row_ref.at[c])


def _gateprep_chunk(g, col_ref, row_ref):
    lane = lax.broadcasted_iota(jnp.int32, (CHUNK, GATE_PAD), 1)
    t = lax.broadcasted_iota(jnp.int32, (CHUNK, GATE_PAD), 0)
    is_forget = (lane % 8) >= HEADS
    is_backward = lane >= 8
    lf = _log_sigmoid(g)
    pre = lf
    suf = lf
    s = 1
    while s < CHUNK:
        pre = pre + jnp.where(t >= s, pltpu.roll(pre, s, 0), 0.0)
        suf = suf + jnp.where(t < CHUNK - s, pltpu.roll(suf, CHUNK - s, 0), 0.0)
        s *= 2
    col = jnp.where(is_forget, jnp.where(is_backward, suf, pre), g)

    cum_f = pltpu.roll(col, GATE_PAD - HEADS, 1)
    diff = col - cum_f
    pmax = diff
    smax = diff
    s = 1
    while s < CHUNK:
        pmax = jnp.maximum(pmax, jnp.where(t >= s, pltpu.roll(pmax, s, 0), -jnp.inf))
        smax = jnp.maximum(smax, jnp.where(t < CHUNK - s, pltpu.roll(smax, CHUNK - s, 0), -jnp.inf))
        s *= 2
    dmax = cum_f + jnp.where(is_backward, smax, pmax)
    col = jnp.where(lane < GATE_COLS, col, pltpu.roll(dmax, GATE_COLS, 1))
    col_ref[...] = col

    col_t = col.T
    row_id = lax.broadcasted_iota(jnp.int32, (CHUNK, CHUNK), 0)
    end_val = jnp.where((row_id % GATE_COLS) >= 8, col_t[:, 0:1], col_t[:, CHUNK - 1:CHUNK])
    row_ref[0:GATE_ROWS, :] = col_t[0:GATE_ROWS, :]
    row_ref[GATE_ROWS:, :] = jnp.broadcast_to(end_val, (CHUNK, CHUNK))[0:GATE_ROWS, :]


def _gateprep_call(gates, bias_row, seq_rows):
    nchunks = seq_rows // CHUNK
    cpt = PREP_ROWS // CHUNK
    return pl.pallas_call(
        _gateprep_kernel,
        grid=(nchunks // cpt,),
        in_specs=[
            pl.BlockSpec((PREP_ROWS, GATE_PAD), lambda i: (i, 0)),
            pl.BlockSpec((1, GATE_PAD), lambda i: (0, 0)),
        ],
        out_specs=[
            pl.BlockSpec((PREP_ROWS, GATE_PAD), lambda i: (i, 0)),
            pl.BlockSpec((cpt, 2 * GATE_ROWS, CHUNK), lambda i: (i, 0, 0)),
        ],
        out_shape=[
            jax.ShapeDtypeStruct((seq_rows, GATE_PAD), F32),
            jax.ShapeDtypeStruct((nchunks, 2 * GATE_ROWS, CHUNK), F32),
        ],
        compiler_params=_params(("arbitrary",)),
        name="gateprep",
    )(gates, bias_row)


class _ScanGeom:
    def __init__(self, batch, t, ctx_len):
        self.batch = batch
        self.nlat = t // SCAN_ROWS
        self.nctx = ctx_len // SCAN_ROWS
        self.steps = self.nlat + self.nctx
        self.seq_rows = batch * (t + ctx_len)

    def chunk_block(self, d, b, j):
        if d == 0:
            ctx_i, lat_i = j, j - self.nctx
        else:
            ctx_i, lat_i = self.nctx - 1 - j, self.steps - 1 - j
        return jnp.where(j < self.nctx,
                         self.batch * self.nlat + b * self.nctx + ctx_i,
                         b * self.nlat + lat_i)


def _headnorm(y, g):
    yc = y - jnp.mean(y, axis=-1, keepdims=True)
    return yc * lax.rsqrt(jnp.mean(yc * yc, axis=-1, keepdims=True) + NORM_EPS) * g


def _causal_mask(d):
    t = lax.broadcasted_iota(jnp.int32, (CHUNK, CHUNK), 0)
    s = lax.broadcasted_iota(jnp.int32, (CHUNK, CHUNK), 1)
    return (t >= s) if d == 0 else (s >= t)


def _mlstm_kernel(*refs, d, final):
    if final:
        (q_ref, kt_ref, v_ref, col_ref, row_ref, hprev_ref, mo_ref, ng_ref, _zero_init,
         out_ref, c_ref, m_ref) = refs
    else:
        q_ref, kt_ref, v_ref, col_ref, row_ref, out_ref, c_ref, m_ref = refs
        hprev_ref = mo_ref = ng_ref = None

    @pl.when(pl.program_id(1) == 0)
    def _():
        c_ref[...] = jnp.zeros_like(c_ref)
        m_ref[...] = jnp.full(m_ref.shape, -jnp.inf, F32)

    for c in _scan_order(d):
        rows = slice(c * CHUNK, (c + 1) * CHUNK)
        _mlstm_chunk(q_ref.at[rows, :], kt_ref.at[c], v_ref.at[rows, :], col_ref.at[rows, :], row_ref.at[c],
                     hprev_ref.at[rows, :] if final else None, mo_ref.at[rows, :] if final else None, ng_ref,
                     out_ref.at[rows, :], c_ref, m_ref, d, final)


def _scan_order(d):
    return range(SCAN_CHUNKS) if d == 0 else range(SCAN_CHUNKS - 1, -1, -1)


def _mlstm_chunk(q_ref, kt_ref, v_ref, col_ref, row_ref, hprev_ref, mo_ref, ng_ref, out_ref, c_ref, m_ref, d, final):
    col = col_ref[...]
    row = row_ref[...]
    mask = _causal_mask(d)
    heads = range(HEADS)
    sls = [slice(h * HEAD_DIM, (h + 1) * HEAD_DIM) for h in heads]

    ones_col = (lax.broadcasted_iota(jnp.int32, (CHUNK, NORM_TILE), 1) == 0).astype(BF16)
    v_ext = [jnp.concatenate([v_ref[:, sls[h]], ones_col], axis=1) for h in heads]

    c_old = [c_ref[h] for h in heads]
    qk = [_dot(q_ref[:, sls[h]], kt_ref[sls[h], :]) for h in heads]
    qc = [_dot(q_ref[:, sls[h]], c_old[h].astype(BF16)) for h in heads]

    i_idx = [d * 8 + h for h in heads]
    f_idx = [d * 8 + HEADS + h for h in heads]
    bc = [col[:, f_idx[h]:f_idx[h] + 1] for h in heads]
    dmax = [col[:, GATE_COLS + i_idx[h]:GATE_COLS + i_idx[h] + 1] for h in heads]
    li_r = [row[i_idx[h]:i_idx[h] + 1, :] for h in heads]
    br = [row[f_idx[h]:f_idx[h] + 1, :] for h in heads]
    b_end = [row[GATE_ROWS + f_idx[h]:GATE_ROWS + f_idx[h] + 1, :] for h in heads]
    a_max = [row[GATE_ROWS + GATE_COLS + i_idx[h]:GATE_ROWS + GATE_COLS + i_idx[h] + 1, :] for h in heads]
    m_prev = [m_ref[h:h + 1, :] for h in heads]
    inter = [bc[h] + m_prev[h][:, 0:1] for h in heads]
    m_t = [jnp.maximum(inter[h], dmax[h]) for h in heads]
    w_inter = [jnp.exp(inter[h] - m_t[h]) for h in heads]
    floor = [jnp.exp(-m_t[h]) for h in heads]
    decay_mat = [jnp.exp(jnp.where(mask, (bc[h] - m_t[h]) + (li_r[h] - br[h]), -jnp.inf)) for h in heads]
    m_new = [jnp.maximum(b_end[h] + m_prev[h], a_max[h]) for h in heads]
    decay = [jnp.exp(b_end[h] + m_prev[h] - m_new[h]) for h in heads]
    w_row = [jnp.exp(b_end[h] - br[h] + li_r[h] - m_new[h]) for h in heads]
    kw_t = [(kt_ref[sls[h], :].astype(F32) * w_row[h]).astype(BF16) for h in heads]

    s = [(qk[h] * decay_mat[h]).astype(BF16) for h in heads]
    sv = [_dot(s[h], v_ext[h]) for h in heads]
    kv = [_dot(kw_t[h], v_ext[h]) for h in heads]
    tot = [sv[h] + w_inter[h] * qc[h] for h in heads]
    scale = [1.0 / jnp.maximum(jnp.abs(tot[h][:, HEAD_DIM:HEAD_DIM + 1]), floor[h]) for h in heads]
    hout = [tot[h][:, 0:HEAD_DIM] * scale[h] for h in heads]

    for h in heads:
        decay_ext = jnp.concatenate([decay[h]] * (c_ref.shape[2] // CHUNK), axis=1)
        c_ref[h] = decay_ext * c_old[h] + kv[h]
        m_ref[h:h + 1, :] = m_new[h]
    for h in heads:
        if final:
            y = _sigmoid(mo_ref[:, sls[h]].astype(F32)) * (hout[h] + hprev_ref[:, sls[h]])
            out_ref[:, sls[h]] = _headnorm(y, ng_ref[:, sls[h]]).astype(out_ref.dtype)
        else:
            out_ref[:, sls[h]] = hout[h]


def _final_branch_io(in_specs, args, hprev, z_gate_spec, z, norm_g, out_rows):
    in_specs += [hprev[1], z_gate_spec, pl.BlockSpec((1, WIDTH), lambda b, j: (0, 0)),
                 pl.BlockSpec(memory_space=pl.ANY)]
    args += [hprev[0], z, norm_g, jnp.zeros((out_rows, WIDTH), BF16)]
    return {len(args) - 1: 0}


def _mlstm_call(z, q, kt, gcol, grow, geom, d, hprev=None, norm_g=None, out_rows=None):
    final = hprev is not None

    def cur(colblk):
        return pl.BlockSpec((SCAN_ROWS, WIDTH), lambda b, j: (geom.chunk_block(d, b, j), colblk))

    in_specs = [
        cur(0),
        pl.BlockSpec((SCAN_CHUNKS, WIDTH, CHUNK), lambda b, j: (geom.chunk_block(d, b, j), 0, 0)),
        cur(COL_MV),
        pl.BlockSpec((SCAN_ROWS, GATE_PAD), lambda b, j: (geom.chunk_block(d, b, j), 0)),
        pl.BlockSpec((SCAN_CHUNKS, 2 * GATE_ROWS, CHUNK), lambda b, j: (geom.chunk_block(d, b, j), 0, 0)),
    ]
    args = [q, kt, z, gcol, grow]
    aliases = {}
    if final:
        aliases = _final_branch_io(in_specs, args, (hprev, cur(0)), cur(COL_MO), z, norm_g, out_rows)
    return pl.pallas_call(
        functools.partial(_mlstm_kernel, d=d, final=final),
        grid=(geom.batch, geom.steps),
        in_specs=in_specs,
        out_specs=cur(0),
        input_output_aliases=aliases,
        out_shape=jax.ShapeDtypeStruct((out_rows if final else geom.seq_rows, WIDTH), BF16 if final else F32),
        scratch_shapes=[
            pltpu.VMEM((HEADS, HEAD_DIM, HEAD_DIM + NORM_TILE), F32),
            pltpu.VMEM((8, CHUNK), F32),
        ],
        compiler_params=_params(("arbitrary", "arbitrary")),
        name="mlstm_bwd" if d else "mlstm_fwd",
    )(*args)


def _ret_kernel(*refs, d, final):
    if final:
        (q_ref, kt_ref, v_ref, lg_ref, hprev_ref, rg_ref, ng_ref, _zero_init,
         out_ref, r_ref, dm_ref, wx_ref, we_ref, dec_ref) = refs
    else:
        q_ref, kt_ref, v_ref, lg_ref, out_ref, r_ref, dm_ref, wx_ref, we_ref, dec_ref = refs

    @pl.when(pl.program_id(1) == 0)
    def _():
        r_ref[...] = jnp.zeros_like(r_ref)
        log_g = _log_sigmoid(lg_ref[...])
        t = lax.broadcasted_iota(jnp.int32, (CHUNK, CHUNK), 0)
        s = lax.broadcasted_iota(jnp.int32, (CHUNK, CHUNK), 1)
        rel = (t - s) if d == 0 else (s - t)
        relf = jnp.maximum(rel, 0).astype(F32)
        tf = t.astype(F32)
        sf = s[0:1, :].astype(F32)
        from_start = (tf + 1.0) if d == 0 else (CHUNK - tf)
        to_end = (CHUNK - 1.0 - sf) if d == 0 else sf
        for h in range(HEADS):
            lg = log_g[:, h:h + 1]
            dm_ref[h] = jnp.where(rel >= 0, jnp.exp(lg * relf), 0.0)
            wx_ref[h] = jnp.exp(lg * from_start)
            we_ref[h:h + 1, :] = jnp.exp(lg * to_end)
            dec_ref[h:h + 1, :] = jnp.broadcast_to(jnp.exp(lg * float(CHUNK)), (1, CHUNK))

    consts = (r_ref, dm_ref, wx_ref, we_ref, dec_ref)
    for c in _scan_order(d):
        rows = slice(c * CHUNK, (c + 1) * CHUNK)
        _ret_chunk(q_ref.at[rows, :], kt_ref.at[c], v_ref.at[rows, :],
                   hprev_ref.at[rows, :] if final else None, rg_ref.at[rows, :] if final else None,
                   ng_ref if final else None, out_ref.at[rows, :], consts, final)


def _ret_chunk(q_ref, kt_ref, v_ref, hprev_ref, rg_ref, ng_ref, out_ref, consts, final):
    r_ref, dm_ref, wx_ref, we_ref, dec_ref = consts
    heads = range(HEADS)
    sls = [slice(h * HEAD_DIM, (h + 1) * HEAD_DIM) for h in heads]
    reps = HEAD_DIM // CHUNK

    r_old = [r_ref[h] for h in heads]
    qk = [_dot(q_ref[:, sls[h]], kt_ref[sls[h], :]) for h in heads]
    qr = [_dot(q_ref[:, sls[h]], r_old[h].astype(BF16)) for h in heads]
    kw_t = [(kt_ref[sls[h], :].astype(F32) * we_ref[h:h + 1, :]).astype(BF16) for h in heads]
    s = [(qk[h] * dm_ref[h]).astype(BF16) for h in heads]
    sv = [_dot(s[h], v_ref[:, sls[h]]) for h in heads]
    kv = [_dot(kw_t[h], v_ref[:, sls[h]]) for h in heads]
    hout = [sv[h] + qr[h] * jnp.concatenate([wx_ref[h]] * reps, axis=1) for h in heads]
    for h in heads:
        r_ref[h] = jnp.concatenate([dec_ref[h:h + 1, :]] * reps, axis=1) * r_old[h] + kv[h]
    for h in heads:
        if final:
            y = _headnorm(hout[h] + hprev_ref[:, sls[h]], ng_ref[:, sls[h]]) * _silu(rg_ref[:, sls[h]].astype(F32))
            out_ref[:, sls[h]] = y.astype(out_ref.dtype)
        else:
            out_ref[:, sls[h]] = hout[h]


def _ret_call(z, q, kt, decay_logit_row, geom, d, hprev=None, norm_g=None, out_rows=None):
    final = hprev is not None

    def cur(colblk):
        return pl.BlockSpec((SCAN_ROWS, WIDTH), lambda b, j: (geom.chunk_block(d, b, j), colblk))

    in_specs = [cur(0),
                pl.BlockSpec((SCAN_CHUNKS, WIDTH, CHUNK), lambda b, j: (geom.chunk_block(d, b, j), 0, 0)),
                cur(COL_RV),
                pl.BlockSpec((1, 128), lambda b, j: (0, 0))]
    args = [q, kt, z, decay_logit_row]
    aliases = {}
    if final:
        aliases = _final_branch_io(in_specs, args, (hprev, cur(0)), cur(COL_RG), z, norm_g, out_rows)
    return pl.pallas_call(
        functools.partial(_ret_kernel, d=d, final=final),
        grid=(geom.batch, geom.steps),
        in_specs=in_specs,
        out_specs=cur(0),
        input_output_aliases=aliases,
        out_shape=jax.ShapeDtypeStruct((out_rows if final else geom.seq_rows, WIDTH), BF16 if final else F32),
        scratch_shapes=[
            pltpu.VMEM((HEADS, HEAD_DIM, HEAD_DIM), F32),
            pltpu.VMEM((HEADS, CHUNK, CHUNK), F32),
            pltpu.VMEM((HEADS, CHUNK, CHUNK), F32),
            pltpu.VMEM((8, CHUNK), F32),
            pltpu.VMEM((8, CHUNK), F32),
        ],
        compiler_params=_params(("arbitrary", "arbitrary")),
        name="ret_bwd" if d else "ret_fwd",
    )(*args)


def _cmlp_kernel(u_ref, v_ref, w_ref, b_ref, out_ref):
    parts = [(slice(c * CHUNK, (c + 1) * CHUNK), slice(g * S_GROUP_DIM, (g + 1) * S_GROUP_DIM), g)
             for c in range(out_ref.shape[0] // CHUNK) for g in range(S_GROUPS)]
    v = [v_ref[r, l].astype(F32) for r, l, _ in parts]
    vc = [x - jnp.mean(x, axis=-1, keepdims=True) for x in v]
    var = [jnp.mean(x * x, axis=-1, keepdims=True) for x in vc]
    vn = [(x * lax.rsqrt(s + NORM_EPS)).astype(BF16) for x, s in zip(vc, var)]
    mixed = [_dot(w_ref[g], x) + b_ref[:, g:g + 1] for x, (_, _, g) in zip(vn, parts)]
    for x, (r, l, _) in zip(mixed, parts):
        out_ref[r, l] = (u_ref[r, l].astype(F32) * x).astype(out_ref.dtype)


def _cmlp_call(z, s_w, s_b_t):
    seq_rows = z.shape[0]
    return pl.pallas_call(
        _cmlp_kernel,
        grid=(seq_rows // PREP_ROWS,),
        in_specs=[
            pl.BlockSpec((PREP_ROWS, WIDTH), lambda i: (i, COL_SU)),
            pl.BlockSpec((PREP_ROWS, WIDTH), lambda i: (i, COL_SV)),
            pl.BlockSpec((S_GROUPS, CHUNK, CHUNK), lambda i: (0, 0, 0)),
            pl.BlockSpec((CHUNK, S_GROUPS), lambda i: (0, 0)),
        ],
        out_specs=pl.BlockSpec((PREP_ROWS, WIDTH), lambda i: (i, 0)),
        out_shape=jax.ShapeDtypeStruct((seq_rows, WIDTH), BF16),
        compiler_params=_params(("arbitrary",)),
        name="chunk_mlp",
    )(z, z, s_w, s_b_t)


def _merge_kernel(ym_ref, yr_ref, ys_ref, wm_ref, wr_ref, ws_ref, gm_ref, gr_ref, gs_ref, out_ref):
    y = (_sigmoid(gm_ref[...].astype(F32)) * _dot(ym_ref[...], wm_ref[...])
         + _sigmoid(gr_ref[...].astype(F32)) * _dot(yr_ref[...], wr_ref[...])
         + _sigmoid(gs_ref[...].astype(F32)) * _dot(ys_ref[...], ws_ref[...]))
    out_ref[...] = y.astype(out_ref.dtype)


def _merge_call(ym, yr, ys, wm, wr, ws, z, d_model, out_rows):
    tm, tn = ROW_TILE, 512
    gate0 = COL_GATES * WIDTH // tn
    per = d_model // tn

    def ybranch():
        return pl.BlockSpec((tm, WIDTH), lambda m, j: (m, 0))

    def wbranch():
        return pl.BlockSpec((WIDTH, tn), lambda m, j: (0, j))

    def gate(i):
        return pl.BlockSpec((tm, tn), lambda m, j: (m, gate0 + i * per + j))

    return pl.pallas_call(
        _merge_kernel,
        grid=(out_rows // tm, per),
        in_specs=[ybranch(), ybranch(), ybranch(), wbranch(), wbranch(), wbranch(), gate(0), gate(1), gate(2)],
        out_specs=pl.BlockSpec((tm, tn), lambda m, j: (m, j)),
        out_shape=jax.ShapeDtypeStruct((out_rows, d_model), BF16),
        compiler_params=_params(("arbitrary", "arbitrary")),
        name="merge",
    )(ym, yr, ys, wm, wr, ws, z, z, z)


def _outproj_kernel(y_ref, w_ref, x_ref, mod_ref, out_ref):
    out_ref[...] = x_ref[...] + mod_ref[0, 2:3, :] * _dot(y_ref[...], w_ref[...])


def _outproj_call(y, w_out, xf, mod_l, owner_of_tile):
    rows, d = y.shape
    tm, tn = ROW_TILE, 512
    return pl.pallas_call(
        _outproj_kernel,
        grid=(rows // tm, d // tn),
        in_specs=[
            pl.BlockSpec((tm, d), lambda m, j: (m, 0)),
            pl.BlockSpec((d, tn), lambda m, j: (0, j)),
            pl.BlockSpec((tm, tn), lambda m, j: (m, j)),
            pl.BlockSpec((1, MOD_ROWS, tn), lambda m, j: (owner_of_tile(m), 0, j)),
        ],
        out_specs=pl.BlockSpec((tm, tn), lambda m, j: (m, j)),
        out_shape=jax.ShapeDtypeStruct((rows, d), F32),
        compiler_params=_params(("arbitrary", "arbitrary")),
        name="outproj",
    )(y, w_out, xf, mod_l)


FFN_OUT_SLAB = 512


def _ffn_kernel(*refs, final):
    if final:
        x_ref, mod_ref, g_ref, wg_ref, wu_ref, wd_ref, fg_ref, out_ref, h_ref = refs
    else:
        x_ref, mod_ref, g_ref, wg_ref, wu_ref, wd_ref, out_ref, h_ref = refs
    f = pl.program_id(1)

    @pl.when(f == 0)
    def _():
        for r0 in range(0, x_ref.shape[0], ROW_SLAB):
            rows = slice(r0, r0 + ROW_SLAB)
            h = _norm_mod(x_ref[rows, :], g_ref[...], mod_ref[0, 3:4, :], mod_ref[0, 4:5, :])
            h_ref[rows, :] = h.astype(BF16)
        out_ref[...] = jnp.zeros_like(out_ref)

    hb = h_ref[...]
    a = (_silu(_dot(hb, wg_ref[...])) * _dot(hb, wu_ref[...])).astype(BF16)
    d = out_ref.shape[1]
    for c0 in range(0, d, FFN_OUT_SLAB):
        out_ref[:, c0:c0 + FFN_OUT_SLAB] += _dot(a, wd_ref[:, c0:c0 + FFN_OUT_SLAB])

    @pl.when(f == pl.num_programs(1) - 1)
    def _():
        for r0 in range(0, x_ref.shape[0], ROW_SLAB):
            rows = slice(r0, r0 + ROW_SLAB)
            x = x_ref[rows, :] + mod_ref[0, 5:6, :] * out_ref[rows, :]
            if final:
                x = x * lax.rsqrt(jnp.mean(x * x, axis=-1, keepdims=True) + NORM_EPS) * fg_ref[...]
            out_ref[rows, :] = x


def _ffn_call(xf, mod_l, g, wg, wu, wd, owner_of_tile, out_rows, final_g=None):
    d = xf.shape[1]
    d_ff = wg.shape[1]
    final = final_g is not None
    tm, tf = ROW_TILE, (256 if final else 512)
    in_specs = [
        pl.BlockSpec((tm, d), lambda m, f: (m, 0)),
        pl.BlockSpec((1, MOD_ROWS, d), lambda m, f: (owner_of_tile(m), 0, 0)),
        pl.BlockSpec((1, d), lambda m, f: (0, 0)),
        pl.BlockSpec((d, tf), lambda m, f: (0, f)),
        pl.BlockSpec((d, tf), lambda m, f: (0, f)),
        pl.BlockSpec((tf, d), lambda m, f: (f, 0)),
    ]
    args = [xf, mod_l, g, wg, wu, wd]
    if final:
        in_specs.append(pl.BlockSpec((1, d), lambda m, f: (0, 0)))
        args.append(final_g)
    return pl.pallas_call(
        functools.partial(_ffn_kernel, final=final),
        grid=(out_rows // tm, d_ff // tf),
        in_specs=in_specs,
        out_specs=pl.BlockSpec((tm, d), lambda m, f: (m, 0)),
        out_shape=jax.ShapeDtypeStruct((out_rows, d), F32),
        scratch_shapes=[pltpu.VMEM((tm, d), BF16)],
        compiler_params=_params(("arbitrary", "arbitrary")),
        name="ffn",
    )(*args)


def _rope_tables(t):
    rows = t // GRID_W
    half = HEAD_DIM // 4
    freq = ROPE_BASE ** (-jnp.arange(half, dtype=F32) / half)
    ang_r = jnp.arange(rows, dtype=F32)[:, None] * freq
    ang_c = jnp.arange(GRID_W, dtype=F32)[:, None] * freq
    ang = jnp.concatenate([jnp.broadcast_to(ang_r[:, None, :], (rows, GRID_W, half)),
                           jnp.broadcast_to(ang_c[None, :, :], (rows, GRID_W, half))], axis=-1)
    ang = ang.reshape(rows * GRID_W, 2 * half)
    return jnp.cos(ang), jnp.sin(ang)


def kernel(x, c, ctx, c_ctx, ada_w, ada_b, norm_mix_g, norm_ffn_g, w_in, m_gate_b, m_conv_w, m_norm_g, r_decay_logit, r_norm_g, s_w, s_b, w_up_m, w_up_r, w_up_s, w_out, ffn_w_gate, ffn_w_up, ffn_w_down, final_norm_g):
    batch, t, d_model = x.shape
    ctx_len = ctx.shape[1]
    depth = ada_w.shape[0]
    lat_rows = batch * t
    ctx_rows = batch * ctx_len
    assert t % ROW_TILE == 0 and ctx_rows <= ROW_TILE and batch + 1 <= MOD_ROWS
    assert t % PREP_ROWS == 0 and ctx_len % PREP_ROWS == 0 and t % GRID_W == 0
    assert t % SCAN_ROWS == 0 and ctx_len % SCAN_ROWS == 0
    geom = _ScanGeom(batch, t, ctx_len)
    seq_rows = geom.seq_rows
    tiles_per_batch = t // ROW_TILE
    all_rows = lat_rows + ROW_TILE

    def owner_of_tile(m):
        return jnp.minimum(m // tiles_per_batch, batch)

    xf = jnp.concatenate([x.reshape(lat_rows, d_model), ctx.reshape(ctx_rows, d_model),
                          jnp.zeros((ROW_TILE - ctx_rows, d_model), F32)], axis=0)

    cond = jnp.concatenate([c, c_ctx[None, :], jnp.zeros((MOD_ROWS - batch - 1, d_model), F32)], axis=0)
    mod = _ada_call(cond.T, ada_w, ada_b, batch + 1)
    mod = mod.reshape(depth, MOD_ROWS, 6, d_model)
    mod = jnp.pad(mod, ((0, 0), (0, 0), (0, MOD_ROWS - 6), (0, 0)))

    cos, sin = _rope_tables(t)

    g0 = 4 * WIDTH
    w_a = w_in[:, :, :g0].astype(BF16)
    w_b = w_in[:, :, g0 + GATE_COLS:].astype(BF16)
    w_gate = jnp.pad(w_in[:, :, g0:g0 + GATE_COLS], ((0, 0), (0, 0), (0, GATE_PAD - GATE_COLS))).astype(BF16)
    gate_bias = jnp.pad(m_gate_b.reshape(depth, 1, GATE_COLS), ((0, 0), (0, 0), (0, GATE_PAD - GATE_COLS)))
    conv_w = jnp.pad(m_conv_w, ((0, 0), (0, MOD_ROWS - CONV_TAPS), (0, 0)))
    decay_rows = jnp.pad(jnp.concatenate([r_decay_logit, r_decay_logit], axis=-1),
                         ((0, 0), (0, 0), (0, 128 - 2 * HEADS)))
    s_b_t = jnp.swapaxes(s_b, 1, 2)

    for l in range(depth):
        last = l == depth - 1
        z, gates = _inproj_call(xf, mod[l], norm_mix_g[l][None, :], w_a[l], w_b[l], w_gate[l], owner_of_tile)

        mq, mkt, rq, rkt = _prep_call(z, conv_w[l], cos, sin, seq_rows, lat_rows, t, ctx_len)
        gcol, grow = _gateprep_call(gates, gate_bias[l], seq_rows)
        hm = _mlstm_call(z, mq, mkt, gcol, grow, geom, 0)
        ym = _mlstm_call(z, mq, mkt, gcol, grow, geom, 1, hprev=hm, norm_g=m_norm_g[l][None, :], out_rows=all_rows)

        hr = _ret_call(z, rq, rkt, decay_rows[l, 0][None, :], geom, 0)
        yr = _ret_call(z, rq, rkt, decay_rows[l, 1][None, :], geom, 1, hprev=hr, norm_g=r_norm_g[l][None, :],
                       out_rows=all_rows)

        ys = _cmlp_call(z, s_w[l].astype(BF16), s_b_t[l])

        rows_out = lat_rows if last else all_rows
        y = _merge_call(ym, yr, ys, w_up_m[l].astype(BF16), w_up_r[l].astype(BF16), w_up_s[l].astype(BF16),
                        z, d_model, rows_out)
        xf = _outproj_call(y, w_out[l].astype(BF16), xf, mod[l], owner_of_tile)
        xf = _ffn_call(xf, mod[l], norm_ffn_g[l][None, :], ffn_w_gate[l].astype(BF16), ffn_w_up[l].astype(BF16),
                       ffn_w_down[l].astype(BF16), owner_of_tile, rows_out,
                       final_g=final_norm_g[None, :] if last else None)
    return xf.reshape(batch, t, d_model)
```

```python
import functools

import jax
import jax.numpy as jnp
from jax import lax
from jax.experimental import pallas as pl
from jax.experimental.pallas import tpu as pltpu

F32 = jnp.float32
BF16 = jnp.bfloat16

CHUNK = 128
GRID_W = 64
HEADS = 4
HEAD_DIM = 256
WIDTH = HEADS * HEAD_DIM
CONV_TAPS = 5
GATE_COLS = 16
GATE_PAD = 128
NORM_TILE = 128
GATE_ROWS = 32
S_GROUPS = 8
S_GROUP_DIM = 128
ROPE_BASE = 10000.0
NORM_EPS = 1e-6
HALO = 16
PREP_ROWS = 256
SCAN_CHUNKS = 2
SCAN_ROWS = SCAN_CHUNKS * CHUNK
MOD_ROWS = 8

ROW_TILE = 1024
ROW_SLAB = 256
VMEM_LIMIT = 56 * 1024 * 1024

COL_MQ, COL_MK, COL_MV, COL_MO, COL_RQ, COL_RK, COL_RV, COL_RG, COL_SU, COL_SV = range(10)
COL_GATES = 10


def _sigmoid(x):
    return 0.5 * jnp.tanh(0.5 * x) + 0.5


def _silu(x):
    return x * _sigmoid(x)


def _log_sigmoid(x):
    return jnp.minimum(x, 0.0) - jnp.log(1.0 + jnp.exp(-jnp.abs(x)))


def _dot(a, b):
    return jnp.dot(a, b, preferred_element_type=F32)


def _params(sem, vmem=VMEM_LIMIT):
    return pltpu.CompilerParams(dimension_semantics=sem, vmem_limit_bytes=vmem)


def _ada_kernel(s_ref, w_ref, b_ref, o_ref, *, owners):
    s = _silu(s_ref[...])
    w = w_ref[0]
    rows = [jnp.sum(w * s[:, m:m + 1], axis=0, keepdims=True) for m in range(owners)]
    rows.append(jnp.zeros((MOD_ROWS - owners, w.shape[1]), F32))
    o_ref[0] = jnp.concatenate(rows, axis=0) + b_ref[0]


def _ada_call(cond_t, ada_w, ada_b, owners):
    depth, d, n = ada_w.shape
    tn = 512
    return pl.pallas_call(
        functools.partial(_ada_kernel, owners=owners),
        grid=(depth, n // tn),
        in_specs=[
            pl.BlockSpec((d, MOD_ROWS), lambda l, j: (0, 0)),
            pl.BlockSpec((1, d, tn), lambda l, j: (l, 0, j)),
            pl.BlockSpec((1, 1, tn), lambda l, j: (l, 0, j)),
        ],
        out_specs=pl.BlockSpec((1, MOD_ROWS, tn), lambda l, j: (l, 0, j)),
        out_shape=jax.ShapeDtypeStruct((depth, MOD_ROWS, n), F32),
        compiler_params=_params(("arbitrary", "arbitrary")),
        name="adaln",
    )(cond_t, ada_w, ada_b.reshape(depth, 1, n))


def _norm_mod(x, g, shift, scale):
    y = x * lax.rsqrt(jnp.mean(x * x, axis=-1, keepdims=True) + NORM_EPS) * g
    return y * (1.0 + scale) + shift


def _inproj_kernel(x_ref, mod_ref, g_ref, wa_ref, wb_ref, wg_ref, z_ref, gate_ref, hn_ref, *, tiles_a):
    j = pl.program_id(1)

    @pl.when(j == 0)
    def _():
        for r0 in range(0, x_ref.shape[0], ROW_SLAB):
            rows = slice(r0, r0 + ROW_SLAB)
            h = _norm_mod(x_ref[rows, :], g_ref[...], mod_ref[0, 0:1, :], mod_ref[0, 1:2, :])
            hb = h.astype(BF16)
            hn_ref[rows, :] = hb
            gate_ref[rows, :] = _dot(hb, wg_ref[...])

    @pl.when(j < tiles_a)
    def _():
        z_ref[...] = _dot(hn_ref[...], wa_ref[...]).astype(z_ref.dtype)

    @pl.when(j >= tiles_a)
    def _():
        z_ref[...] = _dot(hn_ref[...], wb_ref[...]).astype(z_ref.dtype)


def _inproj_call(xf, mod_l, g, w_a, w_b, w_gate, owner_of_tile):
    rows, d = xf.shape
    tm, tn = ROW_TILE, 1024
    tiles_a = w_a.shape[1] // tn
    n = w_a.shape[1] + w_b.shape[1]
    return pl.pallas_call(
        functools.partial(_inproj_kernel, tiles_a=tiles_a),
        grid=(rows // tm, n // tn),
        in_specs=[
            pl.BlockSpec((tm, d), lambda m, j: (m, 0)),
            pl.BlockSpec((1, MOD_ROWS, d), lambda m, j: (owner_of_tile(m), 0, 0)),
            pl.BlockSpec((1, d), lambda m, j: (0, 0)),
            pl.BlockSpec((d, tn), lambda m, j: (0, jnp.minimum(j, tiles_a - 1))),
            pl.BlockSpec((d, tn), lambda m, j: (0, jnp.maximum(j - tiles_a, 0))),
            pl.BlockSpec((d, GATE_PAD), lambda m, j: (0, 0)),
        ],
        out_specs=[
            pl.BlockSpec((tm, tn), lambda m, j: (m, j)),
            pl.BlockSpec((tm, GATE_PAD), lambda m, j: (m, 0)),
        ],
        out_shape=[
            jax.ShapeDtypeStruct((rows, n), BF16),
            jax.ShapeDtypeStruct((rows, GATE_PAD), F32),
        ],
        scratch_shapes=[pltpu.VMEM((tm, d), BF16)],
        compiler_params=_params(("arbitrary", "arbitrary")),
        name="inproj",
    )(xf, mod_l, g, w_a, w_b, w_gate)


def _transpose_chunks(k, kt_ref):
    for c in range(PREP_ROWS // CHUNK):
        for h in range(HEADS):
            blk = k[c * CHUNK:(c + 1) * CHUNK, h * HEAD_DIM:(h + 1) * HEAD_DIM]
            kt_ref[c, h * HEAD_DIM:(h + 1) * HEAD_DIM, :] = blk.T.astype(kt_ref.dtype)


def _prep_kernel(p_ref, cur_ref, nx_ref, cw_ref, r_ref, cos_ref, sin_ref,
                 mq_ref, mkt_ref, rq_ref, rkt_ref, ext_ref, *, lat_tiles, lat_per_seq, ctx_per_seq):
    i = pl.program_id(0)
    is_lat = i < lat_tiles
    pos = jnp.where(is_lat, i % lat_per_seq, (i - lat_tiles) % ctx_per_seq)
    per_seq = jnp.where(is_lat, lat_per_seq, ctx_per_seq)
    halo_zero = jnp.zeros(p_ref.shape, p_ref.dtype)
    ext_ref[0:HALO, :] = jnp.where(pos == 0, halo_zero, p_ref[...])
    ext_ref[HALO:HALO + PREP_ROWS, :] = cur_ref[...]
    ext_ref[HALO + PREP_ROWS:, :] = jnp.where(pos == per_seq - 1, halo_zero, nx_ref[...])

    window = CHUNK + 2 * HALO
    out_row = lax.broadcasted_iota(jnp.int32, (CHUNK, window), 0)
    in_row = lax.broadcasted_iota(jnp.int32, (CHUNK, window), 1)
    centre = CONV_TAPS // 2
    side_taps = [tap for tap in range(CONV_TAPS) if tap != centre]
    select = jnp.concatenate([(in_row == out_row + HALO + tap - centre).astype(BF16) for tap in side_taps], axis=0)
    for half in range(2):
        cols = slice(half * WIDTH, (half + 1) * WIDTH)
        blocks = []
        for r0 in range(0, PREP_ROWS, CHUNK):
            shifted = _dot(select, ext_ref[r0:r0 + window, cols])
            acc = ext_ref[r0 + HALO:r0 + HALO + CHUNK, cols].astype(F32) * cw_ref[centre:centre + 1, cols]
            for n, tap in enumerate(side_taps):
                acc = acc + shifted[n * CHUNK:(n + 1) * CHUNK, :] * cw_ref[tap:tap + 1, cols]
            blocks.append(_silu(acc))
        y = jnp.concatenate(blocks, axis=0)
        if half == 0:
            mq_ref[...] = y.astype(mq_ref.dtype)
        else:
            _transpose_chunks(y * (HEAD_DIM ** -0.5), mkt_ref)

    cos = jnp.where(is_lat, cos_ref[...], 1.0)
    sin = jnp.where(is_lat, sin_ref[...], 0.0)
    half_dim = HEAD_DIM // 2
    for part in range(2):
        pieces = []
        for h in range(HEADS):
            c0 = part * WIDTH + h * HEAD_DIM
            x1 = r_ref[:, c0:c0 + half_dim].astype(F32)
            x2 = r_ref[:, c0 + half_dim:c0 + HEAD_DIM].astype(F32)
            pieces += [x1 * cos - x2 * sin, x2 * cos + x1 * sin]
        rot = jnp.concatenate(pieces, axis=-1)
        if part == 0:
            rq_ref[...] = (rot * (HEAD_DIM ** -0.5)).astype(rq_ref.dtype)
        else:
            _transpose_chunks(rot, rkt_ref)


def _prep_call(z, conv_w, cos, sin, seq_rows, lat_rows, lat_len, ctx_len):
    tiles = seq_rows // PREP_ROWS
    per = PREP_ROWS // HALO
    blocks = seq_rows // HALO
    lat_per_seq = lat_len // PREP_ROWS
    ctx_per_seq = ctx_len // PREP_ROWS
    lat_tiles = lat_rows // PREP_ROWS
    cpt = PREP_ROWS // CHUNK

    def table():
        return pl.BlockSpec((PREP_ROWS, HEAD_DIM // 2), lambda i: (jnp.where(i < lat_tiles, i % lat_per_seq, 0), 0))

    def q_out():
        return pl.BlockSpec((PREP_ROWS, WIDTH), lambda i: (i, 0))

    def kt_out():
        return pl.BlockSpec((cpt, WIDTH, CHUNK), lambda i: (i, 0, 0))

    q_shape = jax.ShapeDtypeStruct((seq_rows, WIDTH), BF16)
    kt_shape = jax.ShapeDtypeStruct((seq_rows // CHUNK, WIDTH, CHUNK), BF16)
    return pl.pallas_call(
        functools.partial(_prep_kernel, lat_tiles=lat_tiles, lat_per_seq=lat_per_seq, ctx_per_seq=ctx_per_seq),
        grid=(tiles,),
        in_specs=[
            pl.BlockSpec((HALO, 2 * WIDTH), lambda i: (jnp.maximum(i * per - 1, 0), 0)),
            pl.BlockSpec((PREP_ROWS, 2 * WIDTH), lambda i: (i, 0)),
            pl.BlockSpec((HALO, 2 * WIDTH), lambda i: (jnp.minimum((i + 1) * per, blocks - 1), 0)),
            pl.BlockSpec((MOD_ROWS, 2 * WIDTH), lambda i: (0, 0)),
            pl.BlockSpec((PREP_ROWS, 2 * WIDTH), lambda i: (i, COL_RQ // 2)),
            table(), table(),
        ],
        out_specs=[q_out(), kt_out(), q_out(), kt_out()],
        out_shape=[q_shape, kt_shape, q_shape, kt_shape],
        scratch_shapes=[pltpu.VMEM((PREP_ROWS + 2 * HALO, 2 * WIDTH), BF16)],
        compiler_params=_params(("arbitrary",)),
        name="qkprep",
    )(z, z, z, conv_w, z, cos, sin)


def _gateprep_kernel(g_ref, b_ref, col_ref, row_ref):
    for c in range(col_ref.shape[0] // CHUNK):
        rows = slice(c * CHUNK, (c + 1) * CHUNK)
        _gateprep_chunk(g_ref[rows, :] + b_ref[...], col_ref.at[rows, :], row_ref.at[c])


def _gateprep_chunk(g, col_ref, row_ref):
    lane = lax.broadcasted_iota(jnp.int32, (CHUNK, GATE_PAD), 1)
    t = lax.broadcasted_iota(jnp.int32, (CHUNK, GATE_PAD), 0)
    is_forget = (lane % 8) >= HEADS
    is_backward = lane >= 8
    lf = _log_sigmoid(g)
    pre = lf
    suf = lf
    s = 1
    while s < CHUNK:
        pre = pre + jnp.where(t >= s, pltpu.roll(pre, s, 0), 0.0)
        suf = suf + jnp.where(t < CHUNK - s, pltpu.roll(suf, CHUNK - s, 0), 0.0)
        s *= 2
    col = jnp.where(is_forget, jnp.where(is_backward, suf, pre), g)

    cum_f = pltpu.roll(col, GATE_PAD - HEADS, 1)
    diff = col - cum_f
    pmax = diff
    smax = diff
    s = 1
    while s < CHUNK:
        pmax = jnp.maximum(pmax, jnp.where(t >= s, pltpu.roll(pmax, s, 0), -jnp.inf))
        smax = jnp.maximum(smax, jnp.where(t < CHUNK - s, pltpu.roll(smax, CHUNK - s, 0), -jnp.inf))
        s *= 2
    dmax = cum_f + jnp.where(is_backward, smax, pmax)
    col = jnp.where(lane < GATE_COLS, col, pltpu.roll(dmax, GATE_COLS, 1))
    col_ref[...] = col

    col_t = col.T
    row_id = lax.broadcasted_iota(jnp.int32, (CHUNK, CHUNK), 0)
    end_val = jnp.where((row_id % GATE_COLS) >= 8, col_t[:, 0:1], col_t[:, CHUNK - 1:CHUNK])
    row_ref[0:GATE_ROWS, :] = col_t[0:GATE_ROWS, :]
    row_ref[GATE_ROWS:, :] = jnp.broadcast_to(end_val, (CHUNK, CHUNK))[0:GATE_ROWS, :]


def _gateprep_call(gates, bias_row, seq_rows):
    nchunks = seq_rows // CHUNK
    cpt = PREP_ROWS // CHUNK
    return pl.pallas_call(
        _gateprep_kernel,
        grid=(nchunks // cpt,),
        in_specs=[
            pl.BlockSpec((PREP_ROWS, GATE_PAD), lambda i: (i, 0)),
            pl.BlockSpec((1, GATE_PAD), lambda i: (0, 0)),
        ],
        out_specs=[
            pl.BlockSpec((PREP_ROWS, GATE_PAD), lambda i: (i, 0)),
            pl.BlockSpec((cpt, 2 * GATE_ROWS, CHUNK), lambda i: (i, 0, 0)),
        ],
        out_shape=[
            jax.ShapeDtypeStruct((seq_rows, GATE_PAD), F32),
            jax.ShapeDtypeStruct((nchunks, 2 * GATE_ROWS, CHUNK), F32),
        ],
        compiler_params=_params(("arbitrary",)),
        name="gateprep",
    )(gates, bias_row)


class _ScanGeom:
    def __init__(self, batch, t, ctx_len):
        self.batch = batch
        self.nlat = t // SCAN_ROWS
        self.nctx = ctx_len // SCAN_ROWS
        self.steps = self.nlat + self.nctx
        self.seq_rows = batch * (t + ctx_len)

    def chunk_block(self, d, b, j):
        if d == 0:
            ctx_i, lat_i = j, j - self.nctx
        else:
            ctx_i, lat_i = self.nctx - 1 - j, self.steps - 1 - j
        return jnp.where(j < self.nctx,
                         self.batch * self.nlat + b * self.nctx + ctx_i,
                         b * self.nlat + lat_i)


def _headnorm(y, g):
    yc = y - jnp.mean(y, axis=-1, keepdims=True)
    return yc * lax.rsqrt(jnp.mean(yc * yc, axis=-1, keepdims=True) + NORM_EPS) * g


def _causal_mask(d):
    t = lax.broadcasted_iota(jnp.int32, (CHUNK, CHUNK), 0)
    s = lax.broadcasted_iota(jnp.int32, (CHUNK, CHUNK), 1)
    return (t >= s) if d == 0 else (s >= t)


def _mlstm_kernel(*refs, d, final):
    if final:
        (q_ref, kt_ref, v_ref, col_ref, row_ref, hprev_ref, mo_ref, ng_ref, _zero_init,
         out_ref, c_ref, m_ref) = refs
    else:
        q_ref, kt_ref, v_ref, col_ref, row_ref, out_ref, c_ref, m_ref = refs
        hprev_ref = mo_ref = ng_ref = None

    @pl.when(pl.program_id(1) == 0)
    def _():
        c_ref[...] = jnp.zeros_like(c_ref)
        m_ref[...] = jnp.full(m_ref.shape, -jnp.inf, F32)

    for c in _scan_order(d):
        rows = slice(c * CHUNK, (c + 1) * CHUNK)
        _mlstm_chunk(q_ref.at[rows, :], kt_ref.at[c], v_ref.at[rows, :], col_ref.at[rows, :], row_ref.at[c],
                     hprev_ref.at[rows, :] if final else None, mo_ref.at[rows, :] if final else None, ng_ref,
                     out_ref.at[rows, :], c_ref, m_ref, d, final)


def _scan_order(d):
    return range(SCAN_CHUNKS) if d == 0 else range(SCAN_CHUNKS - 1, -1, -1)


def _mlstm_chunk(q_ref, kt_ref, v_ref, col_ref, row_ref, hprev_ref, mo_ref, ng_ref, out_ref, c_ref, m_ref, d, final):
    col = col_ref[...]
    row = row_ref[...]
    mask = _causal_mask(d)
    heads = range(HEADS)
    sls = [slice(h * HEAD_DIM, (h + 1) * HEAD_DIM) for h in heads]

    ones_col = (lax.broadcasted_iota(jnp.int32, (CHUNK, NORM_TILE), 1) == 0).astype(BF16)
    v_ext = [jnp.concatenate([v_ref[:, sls[h]], ones_col], axis=1) for h in heads]

    c_old = [c_ref[h] for h in heads]
    qk = [_dot(q_ref[:, sls[h]], kt_ref[sls[h], :]) for h in heads]
    qc = [_dot(q_ref[:, sls[h]], c_old[h].astype(BF16)) for h in heads]

    i_idx = [d * 8 + h for h in heads]
    f_idx = [d * 8 + HEADS + h for h in heads]
    bc = [col[:, f_idx[h]:f_idx[h] + 1] for h in heads]
    dmax = [col[:, GATE_COLS + i_idx[h]:GATE_COLS + i_idx[h] + 1] for h in heads]
    li_r = [row[i_idx[h]:i_idx[h] + 1, :] for h in heads]
    br = [row[f_idx[h]:f_idx[h] + 1, :] for h in heads]
    b_end = [row[GATE_ROWS + f_idx[h]:GATE_ROWS + f_idx[h] + 1, :] for h in heads]
    a_max = [row[GATE_ROWS + GATE_COLS + i_idx[h]:GATE_ROWS + GATE_COLS + i_idx[h] + 1, :] for h in heads]
    m_prev = [m_ref[h:h + 1, :] for h in heads]
    inter = [bc[h] + m_prev[h][:, 0:1] for h in heads]
    m_t = [jnp.maximum(inter[h], dmax[h]) for h in heads]
    w_inter = [jnp.exp(inter[h] - m_t[h]) for h in heads]
    floor = [jnp.exp(-m_t[h]) for h in heads]
    decay_mat = [jnp.exp(jnp.where(mask, (bc[h] - m_t[h]) + (li_r[h] - br[h]), -jnp.inf)) for h in heads]
    m_new = [jnp.maximum(b_end[h] + m_prev[h], a_max[h]) for h in heads]
    decay = [jnp.exp(b_end[h] + m_prev[h] - m_new[h]) for h in heads]
    w_row = [jnp.exp(b_end[h] - br[h] + li_r[h] - m_new[h]) for h in heads]
    kw_t = [(kt_ref[sls[h], :].astype(F32) * w_row[h]).astype(BF16) for h in heads]

    s = [(qk[h] * decay_mat[h]).astype(BF16) for h in heads]
    sv = [_dot(s[h], v_ext[h]) for h in heads]
    kv = [_dot(kw_t[h], v_ext[h]) for h in heads]
    tot = [sv[h] + w_inter[h] * qc[h] for h in heads]
    scale = [1.0 / jnp.maximum(jnp.abs(tot[h][:, HEAD_DIM:HEAD_DIM + 1]), floor[h]) for h in heads]
    hout = [tot[h][:, 0:HEAD_DIM] * scale[h] for h in heads]

    for h in heads:
        decay_ext = jnp.concatenate([decay[h]] * (c_ref.shape[2] // CHUNK), axis=1)
        c_ref[h] = decay_ext * c_old[h] + kv[h]
        m_ref[h:h + 1, :] = m_new[h]
    for h in heads:
        if final:
            y = _sigmoid(mo_ref[:, sls[h]].astype(F32)) * (hout[h] + hprev_ref[:, sls[h]])
            out_ref[:, sls[h]] = _headnorm(y, ng_ref[:, sls[h]]).astype(out_ref.dtype)
        else:
            out_ref[:, sls[h]] = hout[h]


def _final_branch_io(in_specs, args, hprev, z_gate_spec, z, norm_g, out_rows):
    in_specs += [hprev[1], z_gate_spec, pl.BlockSpec((1, WIDTH), lambda b, j: (0, 0)),
                 pl.BlockSpec(memory_space=pl.ANY)]
    args += [hprev[0], z, norm_g, jnp.zeros((out_rows, WIDTH), BF16)]
    return {len(args) - 1: 0}


def _mlstm_call(z, q, kt, gcol, grow, geom, d, hprev=None, norm_g=None, out_rows=None):
    final = hprev is not None

    def cur(colblk):
        return pl.BlockSpec((SCAN_ROWS, WIDTH), lambda b, j: (geom.chunk_block(d, b, j), colblk))

    in_specs = [
        cur(0),
        pl.BlockSpec((SCAN_CHUNKS, WIDTH, CHUNK), lambda b, j: (geom.chunk_block(d, b, j), 0, 0)),
        cur(COL_MV),
        pl.BlockSpec((SCAN_ROWS, GATE_PAD), lambda b, j: (geom.chunk_block(d, b, j), 0)),
        pl.BlockSpec((SCAN_CHUNKS, 2 * GATE_ROWS, CHUNK), lambda b, j: (geom.chunk_block(d, b, j), 0, 0)),
    ]
    args = [q, kt, z, gcol, grow]
    aliases = {}
    if final:
        aliases = _final_branch_io(in_specs, args, (hprev, cur(0)), cur(COL_MO), z, norm_g, out_rows)
    return pl.pallas_call(
        functools.partial(_mlstm_kernel, d=d, final=final),
        grid=(geom.batch, geom.steps),
        in_specs=in_specs,
        out_specs=cur(0),
        input_output_aliases=aliases,
        out_shape=jax.ShapeDtypeStruct((out_rows if final else geom.seq_rows, WIDTH), BF16 if final else F32),
        scratch_shapes=[
            pltpu.VMEM((HEADS, HEAD_DIM, HEAD_DIM + NORM_TILE), F32),
            pltpu.VMEM((8, CHUNK), F32),
        ],
        compiler_params=_params(("arbitrary", "arbitrary")),
        name="mlstm_bwd" if d else "mlstm_fwd",
    )(*args)


def _ret_kernel(*refs, d, final):
    if final:
        (q_ref, kt_ref, v_ref, lg_ref, hprev_ref, rg_ref, ng_ref, _zero_init,
         out_ref, r_ref, dm_ref, wx_ref, we_ref, dec_ref) = refs
    else:
        q_ref, kt_ref, v_ref, lg_ref, out_ref, r_ref, dm_ref, wx_ref, we_ref, dec_ref = refs

    @pl.when(pl.program_id(1) == 0)
    def _():
        r_ref[...] = jnp.zeros_like(r_ref)
        log_g = _log_sigmoid(lg_ref[...])
        t = lax.broadcasted_iota(jnp.int32, (CHUNK, CHUNK), 0)
        s = lax.broadcasted_iota(jnp.int32, (CHUNK, CHUNK), 1)
        rel = (t - s) if d == 0 else (s - t)
        relf = jnp.maximum(rel, 0).astype(F32)
        tf = t.astype(F32)
        sf = s[0:1, :].astype(F32)
        from_start = (tf + 1.0) if d == 0 else (CHUNK - tf)
        to_end = (CHUNK - 1.0 - sf) if d == 0 else sf
        for h in range(HEADS):
            lg = log_g[:, h:h + 1]
            dm_ref[h] = jnp.where(rel >= 0, jnp.exp(lg * relf), 0.0)
            wx_ref[h] = jnp.exp(lg * from_start)
            we_ref[h:h + 1, :] = jnp.exp(lg * to_end)
            dec_ref[h:h + 1, :] = jnp.broadcast_to(jnp.exp(lg * float(CHUNK)), (1, CHUNK))

    consts = (r_ref, dm_ref, wx_ref, we_ref, dec_ref)
    for c in _scan_order(d):
        rows = slice(c * CHUNK, (c + 1) * CHUNK)
        _ret_chunk(q_ref.at[rows, :], kt_ref.at[c], v_ref.at[rows, :],
                   hprev_ref.at[rows, :] if final else None, rg_ref.at[rows, :] if final else None,
                   ng_ref if final else None, out_ref.at[rows, :], consts, final)


def _ret_chunk(q_ref, kt_ref, v_ref, hprev_ref, rg_ref, ng_ref, out_ref, consts, final):
    r_ref, dm_ref, wx_ref, we_ref, dec_ref = consts
    heads = range(HEADS)
    sls = [slice(h * HEAD_DIM, (h + 1) * HEAD_DIM) for h in heads]
    reps = HEAD_DIM // CHUNK

    r_old = [r_ref[h] for h in heads]
    qk = [_dot(q_ref[:, sls[h]], kt_ref[sls[h], :]) for h in heads]
    qr = [_dot(q_ref[:, sls[h]], r_old[h].astype(BF16)) for h in heads]
    kw_t = [(kt_ref[sls[h], :].astype(F32) * we_ref[h:h + 1, :]).astype(BF16) for h in heads]
    s = [(qk[h] * dm_ref[h]).astype(BF16) for h in heads]
    sv = [_dot(s[h], v_ref[:, sls[h]]) for h in heads]
    kv = [_dot(kw_t[h], v_ref[:, sls[h]]) for h in heads]
    hout = [sv[h] + qr[h] * jnp.concatenate([wx_ref[h]] * reps, axis=1) for h in heads]
    for h in heads:
        r_ref[h] = jnp.concatenate([dec_ref[h:h + 1, :]] * reps, axis=1) * r_old[h] + kv[h]
    for h in heads:
        if final:
            y = _headnorm(hout[h] + hprev_ref[:, sls[h]], ng_ref[:, sls[h]]) * _silu(rg_ref[:, sls[h]].astype(F32))
            out_ref[:, sls[h]] = y.astype(out_ref.dtype)
        else:
            out_ref[:, sls[h]] = hout[h]


def _ret_call(z, q, kt, decay_logit_row, geom, d, hprev=None, norm_g=None, out_rows=None):
    final = hprev is not None

    def cur(colblk):
        return pl.BlockSpec((SCAN_ROWS, WIDTH), lambda b, j: (geom.chunk_block(d, b, j), colblk))

    in_specs = [cur(0),
                pl.BlockSpec((SCAN_CHUNKS, WIDTH, CHUNK), lambda b, j: (geom.chunk_block(d, b, j), 0, 0)),
                cur(COL_RV),
                pl.BlockSpec((1, 128), lambda b, j: (0, 0))]
    args = [q, kt, z, decay_logit_row]
    aliases = {}
    if final:
        aliases = _final_branch_io(in_specs, args, (hprev, cur(0)), cur(COL_RG), z, norm_g, out_rows)
    return pl.pallas_call(
        functools.partial(_ret_kernel, d=d, final=final),
        grid=(geom.batch, geom.steps),
        in_specs=in_specs,
        out_specs=cur(0),
        input_output_aliases=aliases,
        out_shape=jax.ShapeDtypeStruct((out_rows if final else geom.seq_rows, WIDTH), BF16 if final else F32),
        scratch_shapes=[
            pltpu.VMEM((HEADS, HEAD_DIM, HEAD_DIM), F32),
            pltpu.VMEM((HEADS, CHUNK, CHUNK), F32),
            pltpu.VMEM((HEADS, CHUNK, CHUNK), F32),
            pltpu.VMEM((8, CHUNK), F32),
            pltpu.VMEM((8, CHUNK), F32),
        ],
        compiler_params=_params(("arbitrary", "arbitrary")),
        name="ret_bwd" if d else "ret_fwd",
    )(*args)


def _cmlp_kernel(u_ref, v_ref, w_ref, b_ref, out_ref):
    parts = [(slice(c * CHUNK, (c + 1) * CHUNK), slice(g * S_GROUP_DIM, (g + 1) * S_GROUP_DIM), g)
             for c in range(out_ref.shape[0] // CHUNK) for g in range(S_GROUPS)]
    v = [v_ref[r, l].astype(F32) for r, l, _ in parts]
    vc = [x - jnp.mean(x, axis=-1, keepdims=True) for x in v]
    var = [jnp.mean(x * x, axis=-1, keepdims=True) for x in vc]
    vn = [(x * lax.rsqrt(s + NORM_EPS)).astype(BF16) for x, s in zip(vc, var)]
    mixed = [_dot(w_ref[g], x) + b_ref[:, g:g + 1] for x, (_, _, g) in zip(vn, parts)]
    for x, (r, l, _) in zip(mixed, parts):
        out_ref[r, l] = (u_ref[r, l].astype(F32) * x).astype(out_ref.dtype)


def _cmlp_call(z, s_w, s_b_t):
    seq_rows = z.shape[0]
    return pl.pallas_call(
        _cmlp_kernel,
        grid=(seq_rows // PREP_ROWS,),
        in_specs=[
            pl.BlockSpec((PREP_ROWS, WIDTH), lambda i: (i, COL_SU)),
            pl.BlockSpec((PREP_ROWS, WIDTH), lambda i: (i, COL_SV)),
            pl.BlockSpec((S_GROUPS, CHUNK, CHUNK), lambda i: (0, 0, 0)),
            pl.BlockSpec((CHUNK, S_GROUPS), lambda i: (0, 0)),
        ],
        out_specs=pl.BlockSpec((PREP_ROWS, WIDTH), lambda i: (i, 0)),
        out_shape=jax.ShapeDtypeStruct((seq_rows, WIDTH), BF16),
        compiler_params=_params(("arbitrary",)),
        name="chunk_mlp",
    )(z, z, s_w, s_b_t)


def _merge_kernel(ym_ref, yr_ref, ys_ref, wm_ref, wr_ref, ws_ref, gm_ref, gr_ref, gs_ref, out_ref):
    y = (_sigmoid(gm_ref[...].astype(F32)) * _dot(ym_ref[...], wm_ref[...])
         + _sigmoid(gr_ref[...].astype(F32)) * _dot(yr_ref[...], wr_ref[...])
         + _sigmoid(gs_ref[...].astype(F32)) * _dot(ys_ref[...], ws_ref[...]))
    out_ref[...] = y.astype(out_ref.dtype)


def _merge_call(ym, yr, ys, wm, wr, ws, z, d_model, out_rows):
    tm, tn = ROW_TILE, 512
    gate0 = COL_GATES * WIDTH // tn
    per = d_model // tn

    def ybranch():
        return pl.BlockSpec((tm, WIDTH), lambda m, j: (m, 0))

    def wbranch():
        return pl.BlockSpec((WIDTH, tn), lambda m, j: (0, j))

    def gate(i):
        return pl.BlockSpec((tm, tn), lambda m, j: (m, gate0 + i * per + j))

    return pl.pallas_call(
        _merge_kernel,
        grid=(out_rows // tm, per),
        in_specs=[ybranch(), ybranch(), ybranch(), wbranch(), wbranch(), wbranch(), gate(0), gate(1), gate(2)],
        out_specs=pl.BlockSpec((tm, tn), lambda m, j: (m, j)),
        out_shape=jax.ShapeDtypeStruct((out_rows, d_model), BF16),
        compiler_params=_params(("arbitrary", "arbitrary")),
        name="merge",
    )(ym, yr, ys, wm, wr, ws, z, z, z)


def _outproj_kernel(y_ref, w_ref, x_ref, mod_ref, out_ref):
    out_ref[...] = x_ref[...] + mod_ref[0, 2:3, :] * _dot(y_ref[...], w_ref[...])


def _outproj_call(y, w_out, xf, mod_l, owner_of_tile):
    rows, d = y.shape
    tm, tn = ROW_TILE, 512
    return pl.pallas_call(
        _outproj_kernel,
        grid=(rows // tm, d // tn),
        in_specs=[
            pl.BlockSpec((tm, d), lambda m, j: (m, 0)),
            pl.BlockSpec((d, tn), lambda m, j: (0, j)),
            pl.BlockSpec((tm, tn), lambda m, j: (m, j)),
            pl.BlockSpec((1, MOD_ROWS, tn), lambda m, j: (owner_of_tile(m), 0, j)),
        ],
        out_specs=pl.BlockSpec((tm, tn), lambda m, j: (m, j)),
        out_shape=jax.ShapeDtypeStruct((rows, d), F32),
        compiler_params=_params(("arbitrary", "arbitrary")),
        name="outproj",
    )(y, w_out, xf, mod_l)


def _mergeout_kernel(ym_ref, yr_ref, ys_ref, wm_ref, wr_ref, ws_ref, gm_ref, gr_ref, gs_ref,
                     wo_ref, x_ref, mod_ref, out_ref, y_ref, *, nt):
    s = pl.program_id(1)

    @pl.when(s < nt)
    def _():
        y = (_sigmoid(gm_ref[...].astype(F32)) * _dot(ym_ref[...], wm_ref[...])
             + _sigmoid(gr_ref[...].astype(F32)) * _dot(yr_ref[...], wr_ref[...])
             + _sigmoid(gs_ref[...].astype(F32)) * _dot(ys_ref[...], ws_ref[...]))
        y_ref[s] = y.astype(y_ref.dtype)

    @pl.when(s >= nt)
    def _():
        tk = y_ref.shape[2]
        acc = _dot(y_ref[0], wo_ref[0:tk, :])
        for k in range(1, nt):
            acc = acc + _dot(y_ref[k], wo_ref[k * tk:(k + 1) * tk, :])
        out_ref[...] = x_ref[...] + mod_ref[0, 2:3, :] * acc


def _mergeout_call(ym, yr, ys, wm, wr, ws, z, w_out, xf, mod_l, owner_of_tile, out_rows):
    d = xf.shape[1]
    tm, tn = ROW_TILE, 512
    nt = d // tn
    gate0 = COL_GATES * WIDTH // tn

    def first(s):
        return jnp.minimum(s, nt - 1)

    def second(s):
        return jnp.maximum(s - nt, 0)

    def ybranch():
        return pl.BlockSpec((tm, WIDTH), lambda m, s: (m, 0))

    def wbranch():
        return pl.BlockSpec((WIDTH, tn), lambda m, s: (0, first(s)))

    def gate(i):
        return pl.BlockSpec((tm, tn), lambda m, s: (m, gate0 + i * nt + first(s)))

    return pl.pallas_call(
        functools.partial(_mergeout_kernel, nt=nt),
        grid=(out_rows // tm, 2 * nt),
        in_specs=[ybranch(), ybranch(), ybranch(), wbranch(), wbranch(), wbranch(), gate(0), gate(1), gate(2),
                  pl.BlockSpec((d, tn), lambda m, s: (0, second(s))),
                  pl.BlockSpec((tm, tn), lambda m, s: (m, second(s))),
                  pl.BlockSpec((1, MOD_ROWS, tn), lambda m, s: (owner_of_tile(m), 0, second(s)))],
        out_specs=pl.BlockSpec((tm, tn), lambda m, s: (m, second(s))),
        out_shape=jax.ShapeDtypeStruct((out_rows, d), F32),
        scratch_shapes=[pltpu.VMEM((nt, tm, tn), BF16)],
        compiler_params=_params(("arbitrary", "arbitrary")),
        name="mergeout",
    )(ym, yr, ys, wm, wr, ws, z, z, z, w_out, xf, mod_l)


FFN_OUT_SLAB = 512


def _ffn_kernel(*refs, final):
    if final:
        x_ref, mod_ref, g_ref, wg_ref, wu_ref, wd_ref, fg_ref, out_ref, h_ref = refs
    else:
        x_ref, mod_ref, g_ref, wg_ref, wu_ref, wd_ref, out_ref, h_ref = refs
    f = pl.program_id(1)

    @pl.when(f == 0)
    def _():
        for r0 in range(0, x_ref.shape[0], ROW_SLAB):
            rows = slice(r0, r0 + ROW_SLAB)
            h = _norm_mod(x_ref[rows, :], g_ref[...], mod_ref[0, 3:4, :], mod_ref[0, 4:5, :])
            h_ref[rows, :] = h.astype(BF16)
        out_ref[...] = jnp.zeros_like(out_ref)

    hb = h_ref[...]
    a = (_silu(_dot(hb, wg_ref[...])) * _dot(hb, wu_ref[...])).astype(BF16)
    d = out_ref.shape[1]
    for c0 in range(0, d, FFN_OUT_SLAB):
        out_ref[:, c0:c0 + FFN_OUT_SLAB] += _dot(a, wd_ref[:, c0:c0 + FFN_OUT_SLAB])

    @pl.when(f == pl.num_programs(1) - 1)
    def _():
        for r0 in range(0, x_ref.shape[0], ROW_SLAB):
            rows = slice(r0, r0 + ROW_SLAB)
            x = x_ref[rows, :] + mod_ref[0, 5:6, :] * out_ref[rows, :]
            if final:
                x = x * lax.rsqrt(jnp.mean(x * x, axis=-1, keepdims=True) + NORM_EPS) * fg_ref[...]
            out_ref[rows, :] = x


def _ffn_call(xf, mod_l, g, wg, wu, wd, owner_of_tile, out_rows, final_g=None):
    d = xf.shape[1]
    d_ff = wg.shape[1]
    final = final_g is not None
    tm, tf = ROW_TILE, (256 if final else 512)
    in_specs = [
        pl.BlockSpec((tm, d), lambda m, f: (m, 0)),
        pl.BlockSpec((1, MOD_ROWS, d), lambda m, f: (owner_of_tile(m), 0, 0)),
        pl.BlockSpec((1, d), lambda m, f: (0, 0)),
        pl.BlockSpec((d, tf), lambda m, f: (0, f)),
        pl.BlockSpec((d, tf), lambda m, f: (0, f)),
        pl.BlockSpec((tf, d), lambda m, f: (f, 0)),
    ]
    args = [xf, mod_l, g, wg, wu, wd]
    if final:
        in_specs.append(pl.BlockSpec((1, d), lambda m, f: (0, 0)))
        args.append(final_g)
    return pl.pallas_call(
        functools.partial(_ffn_kernel, final=final),
        grid=(out_rows // tm, d_ff // tf),
        in_specs=in_specs,
        out_specs=pl.BlockSpec((tm, d), lambda m, f: (m, 0)),
        out_shape=jax.ShapeDtypeStruct((out_rows, d), F32),
        scratch_shapes=[pltpu.VMEM((tm, d), BF16)],
        compiler_params=_params(("arbitrary", "arbitrary")),
        name="ffn",
    )(*args)


def _rope_tables(t):
    rows = t // GRID_W
    half = HEAD_DIM // 4
    freq = ROPE_BASE ** (-jnp.arange(half, dtype=F32) / half)
    ang_r = jnp.arange(rows, dtype=F32)[:, None] * freq
    ang_c = jnp.arange(GRID_W, dtype=F32)[:, None] * freq
    ang = jnp.concatenate([jnp.broadcast_to(ang_r[:, None, :], (rows, GRID_W, half)),
                           jnp.broadcast_to(ang_c[None, :, :], (rows, GRID_W, half))], axis=-1)
    ang = ang.reshape(rows * GRID_W, 2 * half)
    return jnp.cos(ang), jnp.sin(ang)


def kernel(x, c, ctx, c_ctx, ada_w, ada_b, norm_mix_g, norm_ffn_g, w_in, m_gate_b, m_conv_w, m_norm_g, r_decay_logit, r_norm_g, s_w, s_b, w_up_m, w_up_r, w_up_s, w_out, ffn_w_gate, ffn_w_up, ffn_w_down, final_norm_g):
    batch, t, d_model = x.shape
    ctx_len = ctx.shape[1]
    depth = ada_w.shape[0]
    lat_rows = batch * t
    ctx_rows = batch * ctx_len
    assert t % ROW_TILE == 0 and ctx_rows <= ROW_TILE and batch + 1 <= MOD_ROWS
    assert t % PREP_ROWS == 0 and ctx_len % PREP_ROWS == 0 and t % GRID_W == 0
    assert t % SCAN_ROWS == 0 and ctx_len % SCAN_ROWS == 0
    geom = _ScanGeom(batch, t, ctx_len)
    seq_rows = geom.seq_rows
    tiles_per_batch = t // ROW_TILE
    all_rows = lat_rows + ROW_TILE

    def owner_of_tile(m):
        return jnp.minimum(m // tiles_per_batch, batch)

    xf = jnp.concatenate([x.reshape(lat_rows, d_model), ctx.reshape(ctx_rows, d_model),
                          jnp.zeros((ROW_TILE - ctx_rows, d_model), F32)], axis=0)

    cond = jnp.concatenate([c, c_ctx[None, :], jnp.zeros((MOD_ROWS - batch - 1, d_model), F32)], axis=0)
    mod = _ada_call(cond.T, ada_w, ada_b, batch + 1)
    mod = mod.reshape(depth, MOD_ROWS, 6, d_model)
    mod = jnp.pad(mod, ((0, 0), (0, 0), (0, MOD_ROWS - 6), (0, 0)))

    cos, sin = _rope_tables(t)

    g0 = 4 * WIDTH
    w_a = w_in[:, :, :g0].astype(BF16)
    w_b = w_in[:, :, g0 + GATE_COLS:].astype(BF16)
    w_gate = jnp.pad(w_in[:, :, g0:g0 + GATE_COLS], ((0, 0), (0, 0), (0, GATE_PAD - GATE_COLS))).astype(BF16)
    gate_bias = jnp.pad(m_gate_b.reshape(depth, 1, GATE_COLS), ((0, 0), (0, 0), (0, GATE_PAD - GATE_COLS)))
    conv_w = jnp.pad(m_conv_w, ((0, 0), (0, MOD_ROWS - CONV_TAPS), (0, 0)))
    decay_rows = jnp.pad(jnp.concatenate([r_decay_logit, r_decay_logit], axis=-1),
                         ((0, 0), (0, 0), (0, 128 - 2 * HEADS)))
    s_b_t = jnp.swapaxes(s_b, 1, 2)

    for l in range(depth):
        last = l == depth - 1
        z, gates = _inproj_call(xf, mod[l], norm_mix_g[l][None, :], w_a[l], w_b[l], w_gate[l], owner_of_tile)

        mq, mkt, rq, rkt = _prep_call(z, conv_w[l], cos, sin, seq_rows, lat_rows, t, ctx_len)
        gcol, grow = _gateprep_call(gates, gate_bias[l], seq_rows)
        hm = _mlstm_call(z, mq, mkt, gcol, grow, geom, 0)
        ym = _mlstm_call(z, mq, mkt, gcol, grow, geom, 1, hprev=hm, norm_g=m_norm_g[l][None, :], out_rows=all_rows)

        hr = _ret_call(z, rq, rkt, decay_rows[l, 0][None, :], geom, 0)
        yr = _ret_call(z, rq, rkt, decay_rows[l, 1][None, :], geom, 1, hprev=hr, norm_g=r_norm_g[l][None, :],
                       out_rows=all_rows)

        ys = _cmlp_call(z, s_w[l].astype(BF16), s_b_t[l])

        rows_out = lat_rows if last else all_rows
        xf = _mergeout_call(ym, yr, ys, w_up_m[l].astype(BF16), w_up_r[l].astype(BF16), w_up_s[l].astype(BF16),
                            z, w_out[l].astype(BF16), xf, mod[l], owner_of_tile, rows_out)
        xf = _ffn_call(xf, mod[l], norm_ffn_g[l][None, :], ffn_w_gate[l].astype(BF16), ffn_w_up[l].astype(BF16),
                       ffn_w_down[l].astype(BF16), owner_of_tile, rows_out,
                       final_g=final_norm_g[None, :] if last else None)
    return xf.reshape(batch, t, d_model)
```
